```python
import jax, jax.numpy as jnp
from jax import lax
import numpy as np

D_MODEL = 2048
BATCH = 16
SEQ = 2048
DEPTH = 2

CHUNK = 64
QBLOCK = 128
N_MEM = 256
EPS = 1e-6
NEG = -1e30

FOX_HEADS = 8
FOX_HEAD_DIM = 128
FOX_WIDTH = FOX_HEADS * FOX_HEAD_DIM

MLA_HEADS = 8
MLA_NOPE = 128
MLA_ROPE = 64
MLA_V = 128
MLA_Q_RANK = 512
MLA_KV_RANK = 256
MLA_WIDTH = MLA_HEADS * MLA_V
ROPE_THETA = 10000.0

GLA_HEADS = 4
GLA_KEY_WIDTH = D_MODEL // 2
GLA_VAL_WIDTH = D_MODEL
GLA_DV = GLA_VAL_WIDTH // GLA_HEADS
GLA_GATE_RANK = 16
GLA_TAU = 16.0

MEM_HEADS = 4
MEM_HEAD_DIM = 128
MEM_WIDTH = MEM_HEADS * MEM_HEAD_DIM

EVEN_SIZES = (FOX_WIDTH, FOX_WIDTH, FOX_WIDTH, FOX_HEADS, FOX_WIDTH,
              MLA_Q_RANK, MLA_KV_RANK, MLA_ROPE, MLA_WIDTH,
              MEM_WIDTH, MEM_WIDTH)
ODD_SIZES = (GLA_KEY_WIDTH, GLA_KEY_WIDTH, GLA_VAL_WIDTH, GLA_GATE_RANK, GLA_VAL_WIDTH,
             MEM_WIDTH, MEM_WIDTH)
EVEN_IN = sum(EVEN_SIZES)
ODD_IN = sum(ODD_SIZES)
EVEN_MIX = FOX_WIDTH + MLA_WIDTH + MEM_WIDTH
ODD_MIX = GLA_VAL_WIDTH + MEM_WIDTH
N_EVEN = (DEPTH + 1) // 2
N_ODD = DEPTH // 2

kernel_name = "hybrid_fox_mla_gla_streaming_encoder"


def rms_norm(x, w):
    xf = x.astype(jnp.float32)
    y = xf * lax.rsqrt(jnp.mean(xf * xf, axis=-1, keepdims=True) + EPS)
    return (y * w.astype(jnp.float32)).astype(x.dtype)


def split_cols(t, sizes):
    idx, acc = [], 0
    for s in sizes[:-1]:
        acc += s
        idx.append(acc)
    return jnp.split(t, idx, axis=-1)


def to_heads(t, n_heads):
    b, s, w = t.shape
    return t.reshape(b, s, n_heads, w // n_heads).transpose(0, 2, 1, 3)


def from_heads(t):
    b, h, s, d = t.shape
    return t.transpose(0, 2, 1, 3).reshape(b, s, h * d)


def rotate(x, cos, sin):
    x1, x2 = jnp.split(x.astype(jnp.float32), 2, axis=-1)
    return jnp.concatenate([x1 * cos - x2 * sin, x1 * sin + x2 * cos], axis=-1).astype(x.dtype)


def forgetting_attention(q, k, v, log_f):
    seq, d = q.shape[2], q.shape[3]
    c = jnp.cumsum(log_f, axis=-1)
    scale = d ** -0.5
    outs = []
    for i in range(seq // QBLOCK):
        lo, hi = i * QBLOCK, (i + 1) * QBLOCK
        s = (jnp.einsum('bhqd,bhkd->bhqk', q[:, :, lo:hi], k[:, :, :hi]).astype(jnp.float32) * scale
             + c[:, :, lo:hi, None] - c[:, :, None, :hi])
        mask = jnp.arange(lo, hi)[:, None] >= jnp.arange(hi)[None, :]
        p = jax.nn.softmax(jnp.where(mask, s, NEG), axis=-1).astype(v.dtype)
        outs.append(jnp.einsum('bhqk,bhkd->bhqd', p, v[:, :, :hi]))
    return jnp.concatenate(outs, axis=2)


def mla_attention(q_nope, q_rope, k_nope, k_rope, v):
    seq = q_nope.shape[2]
    scale = (MLA_NOPE + MLA_ROPE) ** -0.5
    chunk_id = jnp.arange(seq) // CHUNK
    outs = []
    for i in range(seq // QBLOCK):
        lo, hi = i * QBLOCK, (i + 1) * QBLOCK
        s = (jnp.einsum('bhqd,bhkd->bhqk', q_nope[:, :, lo:hi], k_nope[:, :, :hi])
             + jnp.einsum('bhqr,bkr->bhqk', q_rope[:, :, lo:hi], k_rope[:, :hi])).astype(jnp.float32) * scale
        mask = chunk_id[lo:hi, None] >= chunk_id[None, :hi]
        p = jax.nn.softmax(jnp.where(mask, s, NEG), axis=-1).astype(v.dtype)
        outs.append(jnp.einsum('bhqk,bhkd->bhqd', p, v[:, :, :hi]))
    return jnp.concatenate(outs, axis=2)


def gla_chunked(q, k, v, log_a):
    b, h, seq, dk = q.shape
    dv = v.shape[-1]
    n = seq // CHUNK
    qf = (q.astype(jnp.float32) * dk ** -0.5).reshape(b, h, n, CHUNK, dk)
    kf = k.astype(jnp.float32).reshape(b, h, n, CHUNK, dk)
    vf = v.astype(jnp.float32).reshape(b, h, n, CHUNK, dv)
    cum = jnp.cumsum(log_a.reshape(b, h, n, CHUNK, dk), axis=3)
    cum_last = cum[:, :, :, -1:, :]
    q_dec = qf * jnp.exp(cum)
    k_inv = kf * jnp.exp(-cum)
    k_end = kf * jnp.exp(cum_last - cum)
    att = jnp.einsum('bhnqd,bhnkd->bhnqk', q_dec, k_inv)
    causal = jnp.tril(jnp.ones((CHUNK, CHUNK), dtype=bool))
    o_intra = jnp.einsum('bhnqk,bhnke->bhnqe', jnp.where(causal, att, 0.0), vf)
    chunk_decay = jnp.exp(cum_last[:, :, :, 0, :])

    def step(state, xs):
        q_c, k_c, v_c, dec_c = xs
        o_c = jnp.einsum('bhqd,bhde->bhqe', q_c, state)
        state = state * dec_c[..., None] + jnp.einsum('bhkd,bhke->bhde', k_c, v_c)
        return state, o_c

    xs = (jnp.moveaxis(q_dec, 2, 0), jnp.moveaxis(k_end, 2, 0),
          jnp.moveaxis(vf, 2, 0), jnp.moveaxis(chunk_decay, 2, 0))
    state0 = jnp.zeros((b, h, dk, dv), jnp.float32)
    _, o_inter = lax.scan(step, state0, xs)
    o = o_intra + jnp.moveaxis(o_inter, 0, 2)
    return o.reshape(b, h, seq, dv).astype(q.dtype)


def memory_branch(q, gate, mem, mem_norm_w, w_mem_kv):
    mk, mv = jnp.split(rms_norm(mem, mem_norm_w) @ w_mem_kv, 2, axis=-1)
    qh, kh, vh = to_heads(q, MEM_HEADS), to_heads(mk, MEM_HEADS), to_heads(mv, MEM_HEADS)
    s = jnp.einsum('bhqd,bhmd->bhqm', qh, kh).astype(jnp.float32) * MEM_HEAD_DIM ** -0.5
    p = jax.nn.softmax(s, axis=-1).astype(vh.dtype)
    o = from_heads(jnp.einsum('bhqm,bhmd->bhqd', p, vh))
    return o * jax.nn.silu(gate)


def even_layer(x, mem, cos, sin, norm_w, w_in, fox_b_f, mla_q_norm_w, mla_w_uq,
               mla_kv_norm_w, mla_w_ukv, mem_norm_w, w_mem_kv, w_out):
    b, s, _ = x.shape
    h = rms_norm(x, norm_w)
    (fq, fk, fv, ff, fg, cq, ckv, kr, mg, mq, memg) = split_cols(h @ w_in, EVEN_SIZES)
    log_f = jax.nn.log_sigmoid((ff + fox_b_f).astype(jnp.float32)).transpose(0, 2, 1)
    o_a = forgetting_attention(to_heads(fq, FOX_HEADS), to_heads(fk, FOX_HEADS),
                               to_heads(fv, FOX_HEADS), log_f)
    o_a = from_heads(o_a) * jax.nn.silu(fg)
    qh = (rms_norm(cq, mla_q_norm_w) @ mla_w_uq).reshape(b, s, MLA_HEADS, MLA_NOPE + MLA_ROPE)
    qh = qh.transpose(0, 2, 1, 3)
    q_nope, q_rope = qh[..., :MLA_NOPE], rotate(qh[..., MLA_NOPE:], cos[:, None], sin[:, None])
    kvh = (rms_norm(ckv, mla_kv_norm_w) @ mla_w_ukv).reshape(b, s, MLA_HEADS, MLA_NOPE + MLA_V)
    kvh = kvh.transpose(0, 2, 1, 3)
    k_nope, v_b = kvh[..., :MLA_NOPE], kvh[..., MLA_NOPE:]
    k_rope = rotate(kr, cos, sin)
    o_b = from_heads(mla_attention(q_nope, q_rope, k_nope, k_rope, v_b)) * jax.nn.silu(mg)
    o_m = memory_branch(mq, memg, mem, mem_norm_w, w_mem_kv)
    return x + jnp.concatenate([o_a, o_b, o_m], axis=-1) @ w_out


def odd_layer(x, mem, norm_w, w_in, gla_w_a2, gla_b_a, gla_norm_w, mem_norm_w, w_mem_kv, w_out):
    h = rms_norm(x, norm_w)
    (gq, gk, gv, ga, gg, mq, memg) = split_cols(h @ w_in, ODD_SIZES)
    log_a = jax.nn.log_sigmoid((ga @ gla_w_a2 + gla_b_a).astype(jnp.float32)) / GLA_TAU
    o_c = gla_chunked(to_heads(gq, GLA_HEADS), to_heads(gk, GLA_HEADS),
                      to_heads(gv, GLA_HEADS), to_heads(log_a, GLA_HEADS))
    o_c = from_heads(rms_norm(o_c, gla_norm_w)) * jax.nn.silu(gg)
    o_m = memory_branch(mq, memg, mem, mem_norm_w, w_mem_kv)
    return x + jnp.concatenate([o_c, o_m], axis=-1) @ w_out


def setup_inputs(seed: int = 0) -> dict:
    key = jax.random.key(seed)
    ks = jax.random.split(key, 24)
    f32 = jnp.float32

    def nrm(k, shape, scale):
        return jax.random.normal(k, shape, f32) * scale

    def gain(k, shape):
        return 1.0 + 0.02 * jax.random.normal(k, shape, f32)

    offset = jax.random.randint(ks[2], (BATCH, 1), 0, 4096, dtype=jnp.int32)
    positions = offset + jnp.arange(SEQ, dtype=jnp.int32)[None, :]
    return {
        "x": jax.random.normal(ks[0], (BATCH, SEQ, D_MODEL), f32),
        "mem": jax.random.normal(ks[1], (BATCH, N_MEM, D_MODEL), f32),
        "positions": positions,
        "e_norm_w": gain(ks[3], (N_EVEN, D_MODEL)),
        "e_w_in": nrm(ks[4], (N_EVEN, D_MODEL, EVEN_IN), D_MODEL ** -0.5),
        "e_fox_b_f": 2.0 + 0.1 * jax.random.normal(ks[5], (N_EVEN, FOX_HEADS), f32),
        "e_mla_q_norm_w": gain(ks[6], (N_EVEN, MLA_Q_RANK)),
        "e_mla_w_uq": nrm(ks[7], (N_EVEN, MLA_Q_RANK, MLA_HEADS * (MLA_NOPE + MLA_ROPE)), MLA_Q_RANK ** -0.5),
        "e_mla_kv_norm_w": gain(ks[8], (N_EVEN, MLA_KV_RANK)),
        "e_mla_w_ukv": nrm(ks[9], (N_EVEN, MLA_KV_RANK, MLA_HEADS * (MLA_NOPE + MLA_V)), MLA_KV_RANK ** -0.5),
        "e_mem_norm_w": gain(ks[10], (N_EVEN, D_MODEL)),
        "e_w_mem_kv": nrm(ks[11], (N_EVEN, D_MODEL, 2 * MEM_WIDTH), D_MODEL ** -0.5),
        "e_w_out": nrm(ks[12], (N_EVEN, EVEN_MIX, D_MODEL), EVEN_MIX ** -0.5),
        "o_norm_w": gain(ks[13], (N_ODD, D_MODEL)),
        "o_w_in": nrm(ks[14], (N_ODD, D_MODEL, ODD_IN), D_MODEL ** -0.5),
        "o_gla_w_a2": nrm(ks[15], (N_ODD, GLA_GATE_RANK, GLA_KEY_WIDTH), GLA_GATE_RANK ** -0.5),
        "o_gla_b_a": 0.1 * jax.random.normal(ks[16], (N_ODD, GLA_KEY_WIDTH), f32),
        "o_gla_norm_w": gain(ks[17], (N_ODD, GLA_DV)),
        "o_mem_norm_w": gain(ks[18], (N_ODD, D_MODEL)),
        "o_w_mem_kv": nrm(ks[19], (N_ODD, D_MODEL, 2 * MEM_WIDTH), D_MODEL ** -0.5),
        "o_w_out": nrm(ks[20], (N_ODD, ODD_MIX, D_MODEL), ODD_MIX ** -0.5),
        "final_norm_w": gain(ks[21], (D_MODEL,)),
    }


def reference(x, mem, positions, e_norm_w, e_w_in, e_fox_b_f, e_mla_q_norm_w, e_mla_w_uq,
              e_mla_kv_norm_w, e_mla_w_ukv, e_mem_norm_w, e_w_mem_kv, e_w_out,
              o_norm_w, o_w_in, o_gla_w_a2, o_gla_b_a, o_gla_norm_w, o_mem_norm_w,
              o_w_mem_kv, o_w_out, final_norm_w):
    inv_freq = 1.0 / (ROPE_THETA ** (jnp.arange(0, MLA_ROPE, 2, dtype=jnp.float32) / MLA_ROPE))
    ang = positions.astype(jnp.float32)[..., None] * inv_freq
    cos, sin = jnp.cos(ang), jnp.sin(ang)
    for layer in range(DEPTH):
        i = layer // 2
        if layer % 2 == 0:
            x = even_layer(x, mem, cos, sin, e_norm_w[i], e_w_in[i], e_fox_b_f[i],
                           e_mla_q_norm_w[i], e_mla_w_uq[i], e_mla_kv_norm_w[i], e_mla_w_ukv[i],
                           e_mem_norm_w[i], e_w_mem_kv[i], e_w_out[i])
        else:
            x = odd_layer(x, mem, o_norm_w[i], o_w_in[i], o_gla_w_a2[i], o_gla_b_a[i],
                          o_gla_norm_w[i], o_mem_norm_w[i], o_w_mem_kv[i], o_w_out[i])
    return rms_norm(x, final_norm_w)
```

```python
import functools

import jax
import jax.numpy as jnp
from jax import lax
from jax.experimental import pallas as pl
from jax.experimental.pallas import tpu as pltpu

F32 = jnp.float32
BF16 = jnp.bfloat16

D_MODEL = 2048
CHUNK = 64
EPS = 1e-6
NEG = -1e30

FOX_HEADS = 8
FOX_HEAD_DIM = 128
FOX_WIDTH = FOX_HEADS * FOX_HEAD_DIM

MLA_HEADS = 8
MLA_NOPE = 128
MLA_ROPE = 64
MLA_V = 128
MLA_Q_RANK = 512
MLA_KV_RANK = 256
MLA_WIDTH = MLA_HEADS * MLA_V
MLA_QK_PAD = 256
ROPE_THETA = 10000.0

GLA_HEADS = 4
GLA_KEY_WIDTH = D_MODEL // 2
GLA_VAL_WIDTH = D_MODEL
GLA_DK = GLA_KEY_WIDTH // GLA_HEADS
GLA_DV = GLA_VAL_WIDTH // GLA_HEADS
GLA_GATE_RANK = 16
GLA_TAU = 16.0

MEM_HEADS = 4
MEM_HEAD_DIM = 128
MEM_WIDTH = MEM_HEADS * MEM_HEAD_DIM

LANES = 128
VMEM_LIMIT_BYTES = 56 * 1024 * 1024
PROJ_ROWS = 1024
OUT_ROWS = 512
PREP_ROWS = 512
ATTN_Q_ROWS = 256


def _params(*sem):
    return pltpu.CompilerParams(dimension_semantics=sem, vmem_limit_bytes=VMEM_LIMIT_BYTES)


def _rms(x, w):
    return x * lax.rsqrt(jnp.mean(x * x, axis=-1, keepdims=True) + EPS) * w


def _log_sigmoid(z):
    return -(jnp.maximum(-z, 0.0) + jnp.log1p(jnp.exp(-jnp.abs(z))))


def _silu(g):
    return g * jax.nn.sigmoid(g)


def _norm_proj_kernel(x_ref, nw_ref, w_ref, cs_ref, *rest, with_tail):
    if with_tail:
        wt_ref, o_ref, ot_ref, h_ref = rest
    else:
        o_ref, h_ref = rest

    @pl.when(pl.program_id(1) == 0)
    def _():
        h = _rms(x_ref[...], nw_ref[...]).astype(BF16)
        h_ref[...] = h
        if with_tail:
            ot_ref[...] = jnp.dot(h, wt_ref[...], preferred_element_type=F32)

    acc = jnp.dot(h_ref[...], w_ref[...], preferred_element_type=F32)
    o_ref[...] = (acc * cs_ref[...]).astype(o_ref.dtype)


def _norm_proj(x, norm_w, w, col_scale, w_tail, *, rows, cols):
    t, k = x.shape
    n = w.shape[1]
    with_tail = w_tail is not None
    in_specs = [
        pl.BlockSpec((rows, k), lambda i, j: (i, 0)),
        pl.BlockSpec((1, k), lambda i, j: (0, 0)),
        pl.BlockSpec((k, cols), lambda i, j: (0, j)),
        pl.BlockSpec((1, cols), lambda i, j: (0, j)),
    ]
    args = [x, norm_w.reshape(1, k), w, col_scale]
    out_shape = [jax.ShapeDtypeStruct((t, n), BF16)]
    out_specs = [pl.BlockSpec((rows, cols), lambda i, j: (i, j))]
    if with_tail:
        in_specs.append(pl.BlockSpec((k, LANES), lambda i, j: (0, 0)))
        args.append(w_tail)
        out_shape.append(jax.ShapeDtypeStruct((t, LANES), F32))
        out_specs.append(pl.BlockSpec((rows, LANES), lambda i, j: (i, 0)))
    outs = pl.pallas_call(
        functools.partial(_norm_proj_kernel, with_tail=with_tail),
        grid=(t // rows, n // cols),
        in_specs=in_specs,
        out_specs=out_specs,
        out_shape=out_shape,
        scratch_shapes=[pltpu.VMEM((rows, k), BF16)],
        compiler_params=_params("parallel", "arbitrary"),
        name="norm_proj",
    )(*args)
    return outs if with_tail else outs[0]


def _fox_gate_kernel(tail_ref, bias_ref, o_ref, *, lane0, heads):
    seq = tail_ref.shape[1]
    logf = _log_sigmoid(tail_ref[0] + bias_ref[...])
    c = logf.T[lane0:lane0 + heads, :]
    lane = lax.broadcasted_iota(jnp.int32, c.shape, 1)
    shift = 1
    while shift < seq:
        c = c + jnp.where(lane >= shift, pltpu.roll(c, shift, axis=1), 0.0)
        shift *= 2
    o_ref[0] = -c


def _fox_gate(tail, bias_row, *, lane0, heads):
    b, s, _ = tail.shape
    return pl.pallas_call(
        functools.partial(_fox_gate_kernel, lane0=lane0, heads=heads),
        grid=(b,),
        in_specs=[pl.BlockSpec((1, s, LANES), lambda i: (i, 0, 0)),
                  pl.BlockSpec((1, LANES), lambda i: (0, 0))],
        out_specs=pl.BlockSpec((1, heads, s), lambda i: (i, 0, 0)),
        out_shape=jax.ShapeDtypeStruct((b, heads, s), F32),
        compiler_params=_params("parallel"),
        name="fox_gate",
    )(tail, bias_row)


def _attn_kernel(q_ref, k_ref, v_ref, g_ref, *rest, mask, q_rows, with_bias):
    if with_bias:
        b_ref, o_ref = rest
    else:
        (o_ref,) = rest
    seq = q_ref.shape[1]
    nt = (((1,), (1,)), ((), ()))
    for i in range(seq // q_rows):
        lo, hi = i * q_rows, (i + 1) * q_rows
        q = q_ref[0, lo:hi, :]
        if mask is None:
            spans = [(0, k_ref.shape[1], False)]
        else:
            spans = ([(0, lo, False)] if lo else []) + [(lo, hi, True)]
        scores = []
        for (a, b, diag) in spans:
            s = lax.dot_general(q, k_ref[0, a:b, :], nt, preferred_element_type=F32)
            if with_bias:
                s = s + b_ref[0, :, a:b]
            if diag:
                row = lax.broadcasted_iota(jnp.int32, s.shape, 0)
                col = lax.broadcasted_iota(jnp.int32, s.shape, 1)
                if mask == "chunk":
                    shift = CHUNK.bit_length() - 1
                    row, col = row >> shift, col >> shift
                s = jnp.where(row >= col, s, NEG)
            scores.append(s)
        m = functools.reduce(jnp.maximum, [jnp.max(s, axis=-1, keepdims=True) for s in scores])
        l = 0.0
        o = 0.0
        for (a, b, _), s in zip(spans, scores):
            p = jnp.exp(s - m)
            l = l + jnp.sum(p, axis=-1, keepdims=True)
            o = o + jnp.dot(p.astype(BF16), v_ref[0, a:b, :], preferred_element_type=F32)
        g = g_ref[0, lo:hi, :].astype(F32)
        o_ref[0, lo:hi, :] = (o / l * _silu(g)).astype(o_ref.dtype)


def _attention(q, k, v, gate, bias, *, heads, dq, dv, q_blk, k_blk, v_blk, g_blk,
               mask, q_rows, name):
    b, s, _ = q.shape
    sk = k.shape[1]
    in_specs = [
        pl.BlockSpec((1, s, dq), lambda i, h: (i, 0, q_blk + h)),
        pl.BlockSpec((1, sk, dq), lambda i, h: (i, 0, k_blk + h)),
        pl.BlockSpec((1, sk, dv), lambda i, h: (i, 0, v_blk + h)),
        pl.BlockSpec((1, s, dv), lambda i, h: (i, 0, g_blk + h)),
    ]
    args = [q, k, v, gate]
    if bias is not None:
        in_specs.append(pl.BlockSpec((1, 1, sk), lambda i, h: (i * heads + h, 0, 0)))
        args.append(bias)
    return pl.pallas_call(
        functools.partial(_attn_kernel, mask=mask, q_rows=q_rows, with_bias=bias is not None),
        grid=(b, heads),
        in_specs=in_specs,
        out_specs=pl.BlockSpec((1, s, dv), lambda i, h: (i, 0, h)),
        out_shape=jax.ShapeDtypeStruct((b, s, heads * dv), BF16),
        compiler_params=_params("parallel", "arbitrary"),
        name=name,
    )(*args)


def _rope(v, cos, sin_signed, lane):
    swapped = jnp.where(lane < MLA_ROPE // 2,
                        pltpu.roll(v, LANES - MLA_ROPE // 2, axis=1),
                        pltpu.roll(v, MLA_ROPE // 2, axis=1))
    return v * cos + swapped * sin_signed


def _mla_prep_kernel(cq_ref, ckv_ref, tail_ref, pos_ref, freq_ref, sign_ref, qnw_ref, kvnw_ref,
                     wuq_ref, wukv_ref, q_ref, k_ref, v_ref):
    rows = cq_ref.shape[0]
    ang = pos_ref[...] * freq_ref[...]
    cos = jnp.cos(ang)
    sin_signed = jnp.sin(ang) * sign_ref[...]
    lane = lax.broadcasted_iota(jnp.int32, (rows, LANES), 1)
    scale = (MLA_NOPE + MLA_ROPE) ** -0.5

    cqn = _rms(cq_ref[...].astype(F32), qnw_ref[...]).astype(BF16)
    q = jnp.dot(cqn, wuq_ref[...], preferred_element_type=F32) * scale
    ckvn = _rms(ckv_ref[...].astype(F32), kvnw_ref[...]).astype(BF16)
    kv = jnp.dot(ckvn, wukv_ref[...], preferred_element_type=F32)
    k_rope = jnp.where(lane < MLA_ROPE, _rope(tail_ref[...], cos, sin_signed, lane), 0.0)
    k_rope = k_rope.astype(BF16)
    for h in range(MLA_HEADS):
        c0 = h * MLA_QK_PAD
        q_ref[:, c0:c0 + MLA_NOPE] = q[:, c0:c0 + MLA_NOPE].astype(BF16)
        q_ref[:, c0 + MLA_NOPE:c0 + MLA_QK_PAD] = _rope(
            q[:, c0 + MLA_NOPE:c0 + MLA_QK_PAD], cos, sin_signed, lane).astype(BF16)
        k_ref[:, c0:c0 + MLA_NOPE] = kv[:, h * MLA_NOPE:(h + 1) * MLA_NOPE].astype(BF16)
        k_ref[:, c0 + MLA_NOPE:c0 + MLA_QK_PAD] = k_rope
    v_ref[...] = kv[:, MLA_HEADS * MLA_NOPE:].astype(BF16)


def _mla_prep(main, tail, pos_b, freq_row, sign_row, q_norm_w, kv_norm_w, w_uq, w_ukv,
              *, cq_col, ckv_col):
    t = main.shape[0]
    rows = PREP_ROWS
    const = lambda shape: pl.BlockSpec(shape, lambda i: (0, 0))
    return pl.pallas_call(
        _mla_prep_kernel,
        grid=(t // rows,),
        in_specs=[
            pl.BlockSpec((rows, MLA_Q_RANK), lambda i: (i, cq_col // MLA_Q_RANK)),
            pl.BlockSpec((rows, MLA_KV_RANK), lambda i: (i, ckv_col // MLA_KV_RANK)),
            pl.BlockSpec((rows, LANES), lambda i: (i, 0)),
            pl.BlockSpec((rows, LANES), lambda i: (i, 0)),
            const((1, LANES)), const((1, LANES)),
            const((1, MLA_Q_RANK)), const((1, MLA_KV_RANK)),
            const(w_uq.shape), const(w_ukv.shape),
        ],
        out_specs=[
            pl.BlockSpec((rows, MLA_HEADS * MLA_QK_PAD), lambda i: (i, 0)),
            pl.BlockSpec((rows, MLA_HEADS * MLA_QK_PAD), lambda i: (i, 0)),
            pl.BlockSpec((rows, MLA_WIDTH), lambda i: (i, 0)),
        ],
        out_shape=[
            jax.ShapeDtypeStruct((t, MLA_HEADS * MLA_QK_PAD), BF16),
            jax.ShapeDtypeStruct((t, MLA_HEADS * MLA_QK_PAD), BF16),
            jax.ShapeDtypeStruct((t, MLA_WIDTH), BF16),
        ],
        compiler_params=_params("parallel"),
        name="mla_prep",
    )(main, main, tail, pos_b, freq_row, sign_row, q_norm_w.reshape(1, -1),
      kv_norm_w.reshape(1, -1), w_uq, w_ukv)


def _gla_kernel(q_ref, k_ref, v_ref, gg_ref, ga_ref, wa_ref, ba_ref, nw_ref, o_ref, state_ref):
    seq = q_ref.shape[1]
    state_ref[...] = jnp.zeros_like(state_ref)
    row = lax.broadcasted_iota(jnp.int32, (CHUNK, CHUNK), 0)
    col = lax.broadcasted_iota(jnp.int32, (CHUNK, CHUNK), 1)
    causal = row >= col
    tri = causal.astype(BF16)
    nt = (((1,), (1,)), ((), ()))
    tn = (((0,), (0,)), ((), ()))

    def body(n, carry):
        r = pl.ds(pl.multiple_of(n * CHUNK, CHUNK), CHUNK)
        z = jnp.dot(ga_ref[0, r, :].astype(BF16), wa_ref[...], preferred_element_type=F32)
        log_a = _log_sigmoid(z + ba_ref[...]) / GLA_TAU
        hi = log_a.astype(BF16)
        lo = (log_a - hi.astype(F32)).astype(BF16)
        cum = (jnp.dot(tri, hi, preferred_element_type=F32)
               + jnp.dot(tri, lo, preferred_element_type=F32))
        cum_last = cum[CHUNK - 1:CHUNK, :]
        q = q_ref[0, r, :].astype(F32)
        k = k_ref[0, r, :].astype(F32)
        v = v_ref[0, r, :]
        q_dec = (q * jnp.exp(cum)).astype(BF16)
        k_inv = (k * jnp.exp(-cum)).astype(BF16)
        k_end = (k * jnp.exp(cum_last - cum)).astype(BF16)
        att = lax.dot_general(q_dec, k_inv, nt, preferred_element_type=F32)
        att = jnp.where(causal, att, 0.0).astype(BF16)
        state = state_ref[...]
        o = (jnp.dot(att, v, preferred_element_type=F32)
             + lax.dot_general(q_dec, state.astype(BF16), nt, preferred_element_type=F32))
        state_ref[...] = (state * jnp.exp(cum_last)
                          + lax.dot_general(v, k_end, tn, preferred_element_type=F32))
        g = gg_ref[0, r, :].astype(F32)
        o_ref[0, r, :] = (_rms(o, nw_ref[...]) * _silu(g)).astype(o_ref.dtype)
        return carry

    lax.fori_loop(0, seq // CHUNK, body, 0)


def _gla(main, tail, w_a2, b_a, norm_w):
    b, s, _ = main.shape
    return pl.pallas_call(
        _gla_kernel,
        grid=(b, GLA_HEADS),
        in_specs=[
            pl.BlockSpec((1, s, GLA_DK), lambda i, h: (i, 0, h)),
            pl.BlockSpec((1, s, GLA_DK), lambda i, h: (i, 0, GLA_HEADS + h)),
            pl.BlockSpec((1, s, GLA_DV), lambda i, h: (i, 0, GLA_HEADS + h)),
            pl.BlockSpec((1, s, GLA_DV), lambda i, h: (i, 0, 2 * GLA_HEADS + h)),
            pl.BlockSpec((1, s, LANES), lambda i, h: (i, 0, 0)),
            pl.BlockSpec((LANES, GLA_DK), lambda i, h: (0, h)),
            pl.BlockSpec((1, GLA_DK), lambda i, h: (0, h)),
            pl.BlockSpec((1, GLA_DV), lambda i, h: (0, 0)),
        ],
        out_specs=pl.BlockSpec((1, s, GLA_DV), lambda i, h: (i, 0, h)),
        out_shape=jax.ShapeDtypeStruct((b, s, GLA_VAL_WIDTH), BF16),
        scratch_shapes=[pltpu.VMEM((GLA_DV, GLA_DK), F32)],
        compiler_params=_params("parallel", "arbitrary"),
        name="gla",
    )(main, main, main, main, tail, w_a2, b_a.reshape(1, -1), norm_w.reshape(1, -1))


def _out_proj_kernel(*refs, n_parts, final_norm):
    parts = refs[:n_parts]
    x_ref, w_ref = refs[n_parts], refs[n_parts + 1]
    if final_norm:
        fw_ref, o_ref = refs[n_parts + 2:]
    else:
        (o_ref,) = refs[n_parts + 2:]
    y = x_ref[...]
    r0 = 0
    for p in parts:
        width = p.shape[1]
        y = y + jnp.dot(p[...], w_ref[r0:r0 + width, :], preferred_element_type=F32)
        r0 += width
    if final_norm:
        y = _rms(y, fw_ref[...])
    o_ref[...] = y


def _out_proj(parts, x, w, final_w):
    t, d = x.shape
    rows = OUT_ROWS
    in_specs = [pl.BlockSpec((rows, p.shape[1]), lambda i: (i, 0)) for p in parts]
    in_specs += [pl.BlockSpec((rows, d), lambda i: (i, 0)),
                 pl.BlockSpec(w.shape, lambda i: (0, 0), pipeline_mode=pl.Buffered(1))]
    args = list(parts) + [x, w]
    if final_w is not None:
        in_specs.append(pl.BlockSpec((1, d), lambda i: (0, 0)))
        args.append(final_w.reshape(1, d))
    return pl.pallas_call(
        functools.partial(_out_proj_kernel, n_parts=len(parts), final_norm=final_w is not None),
        grid=(t // rows,),
        in_specs=in_specs,
        out_specs=pl.BlockSpec((rows, d), lambda i: (i, 0)),
        out_shape=jax.ShapeDtypeStruct((t, d), F32),
        compiler_params=_params("parallel"),
        name="out_proj",
    )(*args)


def _memory_branch(main, mem2d, mem_norm_w, w_mem_kv, *, batch, q_col, g_col):
    n = w_mem_kv.shape[1]
    mem_kv = _norm_proj(mem2d, mem_norm_w, w_mem_kv.astype(BF16), jnp.ones((1, n), F32), None,
                        rows=PROJ_ROWS, cols=n)
    mem_kv = mem_kv.reshape(batch, -1, n)
    return _attention(main, mem_kv, mem_kv, main, None, heads=MEM_HEADS, dq=MEM_HEAD_DIM,
                      dv=MEM_HEAD_DIM, q_blk=q_col // MEM_HEAD_DIM, k_blk=0, v_blk=MEM_HEADS,
                      g_blk=g_col // MEM_HEAD_DIM, mask=None, q_rows=2 * ATTN_Q_ROWS,
                      name="mem_attn")


def _col_scale(widths_and_scales):
    return jnp.concatenate([jnp.full((1, w), s, F32) for w, s in widths_and_scales], axis=1)


def _split_cols(w, sizes):
    out, acc = [], 0
    for s in sizes:
        out.append(w[:, acc:acc + s])
        acc += s
    return out


def _even_layer(x2d, mem2d, pos_b, freq_row, sign_row, batch, norm_w, w_in, fox_b_f,
                mla_q_norm_w, mla_w_uq, mla_kv_norm_w, mla_w_ukv, mem_norm_w, w_mem_kv, w_out):
    t = x2d.shape[0]
    seq = t // batch
    (w_fq, w_fk, w_fv, w_ff, w_fg, w_cq, w_ckv, w_kr, w_mg, w_mq, w_memg) = _split_cols(
        w_in, (FOX_WIDTH, FOX_WIDTH, FOX_WIDTH, FOX_HEADS, FOX_WIDTH, MLA_Q_RANK, MLA_KV_RANK,
               MLA_ROPE, MLA_WIDTH, MEM_WIDTH, MEM_WIDTH))
    w_main = jnp.concatenate([w_fq, w_fk, w_fv, w_fg, w_cq, w_ckv, w_mg, w_mq, w_memg],
                             axis=1).astype(BF16)
    fq_col, fk_col, fv_col, fg_col = 0, FOX_WIDTH, 2 * FOX_WIDTH, 3 * FOX_WIDTH
    cq_col = 4 * FOX_WIDTH
    ckv_col = cq_col + MLA_Q_RANK
    mg_col = ckv_col + MLA_KV_RANK
    mq_col = mg_col + MLA_WIDTH
    memg_col = mq_col + MEM_WIDTH
    n_main = memg_col + MEM_WIDTH
    ff_lane = MLA_ROPE
    w_tail = jnp.concatenate(
        [w_kr, w_ff, jnp.zeros((D_MODEL, LANES - MLA_ROPE - FOX_HEADS), F32)], axis=1).astype(BF16)
    col_scale = _col_scale([(FOX_WIDTH, FOX_HEAD_DIM ** -0.5), (mq_col - FOX_WIDTH, 1.0),
                            (MEM_WIDTH, MEM_HEAD_DIM ** -0.5), (MEM_WIDTH, 1.0)])
    main, tail = _norm_proj(x2d, norm_w, w_main, col_scale, w_tail, rows=PROJ_ROWS, cols=768)
    main3 = main.reshape(batch, seq, n_main)

    bias_row = jnp.zeros((1, LANES), F32).at[0, ff_lane:ff_lane + FOX_HEADS].set(fox_b_f)
    neg_c = _fox_gate(tail.reshape(batch, seq, LANES), bias_row, lane0=ff_lane, heads=FOX_HEADS)
    o_a = _attention(main3, main3, main3, main3, neg_c.reshape(batch * FOX_HEADS, 1, seq),
                     heads=FOX_HEADS, dq=FOX_HEAD_DIM, dv=FOX_HEAD_DIM,
                     q_blk=fq_col // FOX_HEAD_DIM, k_blk=fk_col // FOX_HEAD_DIM,
                     v_blk=fv_col // FOX_HEAD_DIM, g_blk=fg_col // FOX_HEAD_DIM,
                     mask="causal", q_rows=ATTN_Q_ROWS, name="fox_attn")

    pad = jnp.zeros((MLA_Q_RANK, MLA_HEADS, MLA_QK_PAD - MLA_NOPE - MLA_ROPE), F32)
    w_uq = jnp.concatenate([mla_w_uq.reshape(MLA_Q_RANK, MLA_HEADS, MLA_NOPE + MLA_ROPE), pad],
                           axis=2).reshape(MLA_Q_RANK, MLA_HEADS * MLA_QK_PAD).astype(BF16)
    w_ukv = mla_w_ukv.reshape(MLA_KV_RANK, MLA_HEADS, MLA_NOPE + MLA_V)
    w_ukv = jnp.concatenate([w_ukv[:, :, :MLA_NOPE].reshape(MLA_KV_RANK, -1),
                             w_ukv[:, :, MLA_NOPE:].reshape(MLA_KV_RANK, -1)], axis=1).astype(BF16)
    q_cat, k_cat, v_b = _mla_prep(main, tail, pos_b, freq_row, sign_row, mla_q_norm_w,
                                  mla_kv_norm_w, w_uq, w_ukv, cq_col=cq_col, ckv_col=ckv_col)
    o_b = _attention(q_cat.reshape(batch, seq, -1), k_cat.reshape(batch, seq, -1),
                     v_b.reshape(batch, seq, -1), main3, None, heads=MLA_HEADS, dq=MLA_QK_PAD,
                     dv=MLA_V, q_blk=0, k_blk=0, v_blk=0, g_blk=mg_col // MLA_V,
                     mask="chunk", q_rows=ATTN_Q_ROWS, name="mla_attn")

    o_m = _memory_branch(main3, mem2d, mem_norm_w, w_mem_kv, batch=batch, q_col=mq_col,
                         g_col=memg_col)
    parts = [o_a.reshape(t, -1), o_b.reshape(t, -1), o_m.reshape(t, -1)]
    return _out_proj(parts, x2d, w_out.astype(BF16), None)


def _odd_layer(x2d, mem2d, batch, norm_w, w_in, gla_w_a2, gla_b_a, gla_norm_w, mem_norm_w,
               w_mem_kv, w_out, final_w):
    t = x2d.shape[0]
    seq = t // batch
    (w_gq, w_gk, w_gv, w_ga, w_gg, w_mq, w_memg) = _split_cols(
        w_in, (GLA_KEY_WIDTH, GLA_KEY_WIDTH, GLA_VAL_WIDTH, GLA_GATE_RANK, GLA_VAL_WIDTH,
               MEM_WIDTH, MEM_WIDTH))
    w_main = jnp.concatenate([w_gq, w_gk, w_gv, w_gg, w_mq, w_memg], axis=1).astype(BF16)
    mq_col = 2 * GLA_KEY_WIDTH + 2 * GLA_VAL_WIDTH
    memg_col = mq_col + MEM_WIDTH
    n_main = memg_col + MEM_WIDTH
    w_tail = jnp.concatenate([w_ga, jnp.zeros((D_MODEL, LANES - GLA_GATE_RANK), F32)],
                             axis=1).astype(BF16)
    col_scale = _col_scale([(GLA_KEY_WIDTH, GLA_DK ** -0.5), (mq_col - GLA_KEY_WIDTH, 1.0),
                            (MEM_WIDTH, MEM_HEAD_DIM ** -0.5), (MEM_WIDTH, 1.0)])
    main, tail = _norm_proj(x2d, norm_w, w_main, col_scale, w_tail, rows=PROJ_ROWS, cols=1024)
    main3 = main.reshape(batch, seq, n_main)

    w_a2 = jnp.concatenate([gla_w_a2, jnp.zeros((LANES - GLA_GATE_RANK, GLA_KEY_WIDTH), F32)],
                           axis=0).astype(BF16)
    o_c = _gla(main3, tail.reshape(batch, seq, LANES), w_a2, gla_b_a, gla_norm_w)
    o_m = _memory_branch(main3, mem2d, mem_norm_w, w_mem_kv, batch=batch, q_col=mq_col,
                         g_col=memg_col)
    parts = [o_c.reshape(t, -1), o_m.reshape(t, -1)]
    return _out_proj(parts, x2d, w_out.astype(BF16), final_w)


def kernel(x, mem, positions, e_norm_w, e_w_in, e_fox_b_f, e_mla_q_norm_w, e_mla_w_uq, e_mla_kv_norm_w, e_mla_w_ukv, e_mem_norm_w, e_w_mem_kv, e_w_out, o_norm_w, o_w_in, o_gla_w_a2, o_gla_b_a, o_gla_norm_w, o_mem_norm_w, o_w_mem_kv, o_w_out, final_norm_w):
    batch, seq, d = x.shape
    t = batch * seq
    x2d = x.reshape(t, d)
    mem2d = mem.reshape(-1, d)
    inv_freq = 1.0 / (ROPE_THETA ** (jnp.arange(0, MLA_ROPE, 2, dtype=F32) / MLA_ROPE))
    half = MLA_ROPE // 2
    freq_row = jnp.zeros((1, LANES), F32).at[0, :half].set(inv_freq).at[0, half:2 * half].set(inv_freq)
    sign_row = jnp.zeros((1, LANES), F32).at[0, :half].set(-1.0).at[0, half:2 * half].set(1.0)
    pos_b = jnp.broadcast_to(positions.astype(F32).reshape(t, 1), (t, LANES))

    x2d = _even_layer(x2d, mem2d, pos_b, freq_row, sign_row, batch, e_norm_w[0], e_w_in[0],
                      e_fox_b_f[0], e_mla_q_norm_w[0], e_mla_w_uq[0], e_mla_kv_norm_w[0],
                      e_mla_w_ukv[0], e_mem_norm_w[0], e_w_mem_kv[0], e_w_out[0])
    x2d = _odd_layer(x2d, mem2d, batch, o_norm_w[0], o_w_in[0], o_gla_w_a2[0], o_gla_b_a[0],
                     o_gla_norm_w[0], o_mem_norm_w[0], o_w_mem_kv[0], o_w_out[0], final_norm_w)
    return x2d.reshape(batch, seq, d)
```

```python
import functools

import jax
import jax.numpy as jnp
from jax import lax
from jax.experimental import pallas as pl
from jax.experimental.pallas import tpu as pltpu

F32 = jnp.float32
BF16 = jnp.bfloat16

D_MODEL = 2048
CHUNK = 64
EPS = 1e-6
NEG = -1e30

FOX_HEADS = 8
FOX_HEAD_DIM = 128
FOX_WIDTH = FOX_HEADS * FOX_HEAD_DIM

MLA_HEADS = 8
MLA_NOPE = 128
MLA_ROPE = 64
MLA_V = 128
MLA_Q_RANK = 512
MLA_KV_RANK = 256
MLA_WIDTH = MLA_HEADS * MLA_V
MLA_QK_PAD = 256
ROPE_THETA = 10000.0

GLA_HEADS = 4
GLA_KEY_WIDTH = D_MODEL // 2
GLA_VAL_WIDTH = D_MODEL
GLA_DK = GLA_KEY_WIDTH // GLA_HEADS
GLA_DV = GLA_VAL_WIDTH // GLA_HEADS
GLA_GATE_RANK = 16
GLA_TAU = 16.0

MEM_HEADS = 4
MEM_HEAD_DIM = 128
MEM_WIDTH = MEM_HEADS * MEM_HEAD_DIM

LANES = 128
VMEM_LIMIT_BYTES = 56 * 1024 * 1024
PROJ_ROWS = 1024
OUT_ROWS = 512
PREP_ROWS = 512
ATTN_Q_ROWS = 256
ATTN_K_COLS = 256
LOG2E = 1.4426950408889634
GLA_BLOCK_ROWS = 256


def _params(*sem):
    return pltpu.CompilerParams(dimension_semantics=sem, vmem_limit_bytes=VMEM_LIMIT_BYTES)


def _rms(x, w):
    return x * lax.rsqrt(jnp.mean(x * x, axis=-1, keepdims=True) + EPS) * w


def _log_sigmoid(z):
    return jnp.minimum(z, 0.0) - jnp.log(1.0 + jnp.exp(-jnp.abs(z)))


def _silu(g):
    return g * jax.nn.sigmoid(g)


def _norm_proj_kernel(x_ref, nw_ref, w_ref, cs_ref, *rest, with_tail):
    if with_tail:
        wt_ref, o_ref, ot_ref, h_ref = rest
    else:
        o_ref, h_ref = rest

    @pl.when(pl.program_id(1) == 0)
    def _():
        h = _rms(x_ref[...], nw_ref[...]).astype(BF16)
        h_ref[...] = h
        if with_tail:
            ot_ref[...] = jnp.dot(h, wt_ref[...], preferred_element_type=F32)

    acc = jnp.dot(h_ref[...], w_ref[...], preferred_element_type=F32)
    o_ref[...] = (acc * cs_ref[...]).astype(o_ref.dtype)


def _norm_proj(x, norm_w, w, col_scale, w_tail, *, rows, cols):
    t, k = x.shape
    n = w.shape[1]
    with_tail = w_tail is not None
    in_specs = [
        pl.BlockSpec((rows, k), lambda i, j: (i, 0)),
        pl.BlockSpec((1, k), lambda i, j: (0, 0)),
        pl.BlockSpec((k, cols), lambda i, j: (0, j)),
        pl.BlockSpec((1, cols), lambda i, j: (0, j)),
    ]
    args = [x, norm_w.reshape(1, k), w, col_scale]
    out_shape = [jax.ShapeDtypeStruct((t, n), BF16)]
    out_specs = [pl.BlockSpec((rows, cols), lambda i, j: (i, j))]
    if with_tail:
        in_specs.append(pl.BlockSpec((k, LANES), lambda i, j: (0, 0)))
        args.append(w_tail)
        out_shape.append(jax.ShapeDtypeStruct((t, LANES), F32))
        out_specs.append(pl.BlockSpec((rows, LANES), lambda i, j: (i, 0)))
    outs = pl.pallas_call(
        functools.partial(_norm_proj_kernel, with_tail=with_tail),
        grid=(t // rows, n // cols),
        in_specs=in_specs,
        out_specs=out_specs,
        out_shape=out_shape,
        scratch_shapes=[pltpu.VMEM((rows, k), BF16)],
        compiler_params=_params("parallel", "arbitrary"),
        name="norm_proj",
    )(*args)
    return outs if with_tail else outs[0]


def _fox_gate_kernel(tail_ref, bias_ref, o_ref, *, lane0, heads):
    seq = tail_ref.shape[1]
    logf = _log_sigmoid(tail_ref[0] + bias_ref[...])
    c = logf.T[lane0:lane0 + heads, :]
    lane = lax.broadcasted_iota(jnp.int32, c.shape, 1)
    shift = 1
    while shift < seq:
        c = c + jnp.where(lane >= shift, pltpu.roll(c, shift, axis=1), 0.0)
        shift *= 2
    o_ref[0] = c * -LOG2E


def _fox_gate(tail, bias_row, *, lane0, heads):
    b, s, _ = tail.shape
    return pl.pallas_call(
        functools.partial(_fox_gate_kernel, lane0=lane0, heads=heads),
        grid=(b,),
        in_specs=[pl.BlockSpec((1, s, LANES), lambda i: (i, 0, 0)),
                  pl.BlockSpec((1, LANES), lambda i: (0, 0))],
        out_specs=pl.BlockSpec((1, heads, s), lambda i: (i, 0, 0)),
        out_shape=jax.ShapeDtypeStruct((b, heads, s), F32),
        compiler_params=_params("parallel"),
        name="fox_gate",
    )(tail, bias_row)


def _attn_kernel(q_ref, k_ref, v_ref, g_ref, *rest, mask, q_rows, with_bias):
    if with_bias:
        b_ref, o_ref, s_ref, v1_ref = rest
    else:
        o_ref, s_ref, v1_ref = rest
    seq, sk = q_ref.shape[1], k_ref.shape[1]
    dv = v_ref.shape[2]
    v1_ref[:, :dv] = v_ref[0]
    v1_ref[:, dv:] = jnp.ones((sk, dv), BF16)
    k_cols = ATTN_K_COLS
    nt = (((1,), (1,)), ((), ()))

    def lane_slabs(t):
        return [t[:, c:c + LANES] for c in range(0, t.shape[1], LANES)]

    def tiles(i):
        return range(0, sk if mask is None else (i + 1) * q_rows, k_cols)

    def pass1(i):
        lo = i * q_rows
        q = q_ref[0, lo:lo + q_rows, :]
        mx = jnp.full((q_rows, LANES), NEG, F32)
        for a in tiles(i):
            s = lax.dot_general(q, k_ref[0, a:a + k_cols, :], nt, preferred_element_type=F32)
            if with_bias:
                s = s + b_ref[0, :, a:a + k_cols]
            if mask is not None and a + k_cols > lo + 1:
                row = lo + lax.broadcasted_iota(jnp.int32, s.shape, 0)
                col = a + lax.broadcasted_iota(jnp.int32, s.shape, 1)
                if mask == "chunk":
                    shift = CHUNK.bit_length() - 1
                    row, col = row >> shift, col >> shift
                s = jnp.where(row >= col, s, NEG)
            s_ref[i % 2, :, a:a + k_cols] = s
            mx = functools.reduce(jnp.maximum, lane_slabs(s), mx)
        return jnp.max(mx, axis=-1, keepdims=True)

    def pass2(i, m):
        lo = i * q_rows
        o = jnp.zeros((q_rows, 2 * dv), F32)
        for a in tiles(i):
            p = jnp.exp2(s_ref[i % 2, :, a:a + k_cols] - m)
            o = o + jnp.dot(p.astype(BF16), v1_ref[a:a + k_cols, :],
                            preferred_element_type=F32)
        g = g_ref[0, lo:lo + q_rows, :].astype(F32)
        o_ref[0, lo:lo + q_rows, :] = (o[:, :dv] / o[:, dv:] * _silu(g)).astype(o_ref.dtype)

    n_blocks = seq // q_rows
    m = pass1(0)
    for i in range(n_blocks):
        m_next = pass1(i + 1) if i + 1 < n_blocks else None
        pass2(i, m)
        m = m_next


def _attention(q, k, v, gate, bias, *, heads, dq, dv, q_blk, k_blk, v_blk, g_blk,
               mask, q_rows, name):
    b, s, _ = q.shape
    sk = k.shape[1]
    in_specs = [
        pl.BlockSpec((1, s, dq), lambda i, h: (i, 0, q_blk + h)),
        pl.BlockSpec((1, sk, dq), lambda i, h: (i, 0, k_blk + h)),
        pl.BlockSpec((1, sk, dv), lambda i, h: (i, 0, v_blk + h)),
        pl.BlockSpec((1, s, dv), lambda i, h: (i, 0, g_blk + h)),
    ]
    args = [q, k, v, gate]
    if bias is not None:
        in_specs.append(pl.BlockSpec((1, 1, sk), lambda i, h: (i * heads + h, 0, 0)))
        args.append(bias)
    return pl.pallas_call(
        functools.partial(_attn_kernel, mask=mask, q_rows=q_rows, with_bias=bias is not None),
        grid=(b, heads),
        in_specs=in_specs,
        out_specs=pl.BlockSpec((1, s, dv), lambda i, h: (i, 0, h)),
        out_shape=jax.ShapeDtypeStruct((b, s, heads * dv), BF16),
        scratch_shapes=[pltpu.VMEM((2, q_rows, sk), F32), pltpu.VMEM((sk, 2 * dv), BF16)],
        compiler_params=_params("parallel", "arbitrary"),
        name=name,
    )(*args)


def _rope(v, cos, sin_signed, lane):
    swapped = jnp.where(lane < MLA_ROPE // 2,
                        pltpu.roll(v, LANES - MLA_ROPE // 2, axis=1),
                        pltpu.roll(v, MLA_ROPE // 2, axis=1))
    return v * cos + swapped * sin_signed


def _mla_prep_kernel(cq_ref, ckv_ref, tail_ref, pos_ref, freq_ref, sign_ref, qnw_ref, kvnw_ref,
                     wuq_ref, wukv_ref, q_ref, k_ref, v_ref):
    rows = cq_ref.shape[0]
    ang = pos_ref[...] * freq_ref[...]
    cos = jnp.cos(ang)
    sin_signed = jnp.sin(ang) * sign_ref[...]
    lane = lax.broadcasted_iota(jnp.int32, (rows, LANES), 1)
    scale = (MLA_NOPE + MLA_ROPE) ** -0.5 * LOG2E

    cqn = _rms(cq_ref[...].astype(F32), qnw_ref[...]).astype(BF16)
    q = jnp.dot(cqn, wuq_ref[...], preferred_element_type=F32) * scale
    ckvn = _rms(ckv_ref[...].astype(F32), kvnw_ref[...]).astype(BF16)
    kv = jnp.dot(ckvn, wukv_ref[...], preferred_element_type=F32)
    k_rope = jnp.where(lane < MLA_ROPE, _rope(tail_ref[...], cos, sin_signed, lane), 0.0)
    k_rope = k_rope.astype(BF16)
    for h in range(MLA_HEADS):
        c0 = h * MLA_QK_PAD
        q_ref[:, c0:c0 + MLA_NOPE] = q[:, c0:c0 + MLA_NOPE].astype(BF16)
        q_ref[:, c0 + MLA_NOPE:c0 + MLA_QK_PAD] = _rope(
            q[:, c0 + MLA_NOPE:c0 + MLA_QK_PAD], cos, sin_signed, lane).astype(BF16)
        k_ref[:, c0:c0 + MLA_NOPE] = kv[:, h * MLA_NOPE:(h + 1) * MLA_NOPE].astype(BF16)
        k_ref[:, c0 + MLA_NOPE:c0 + MLA_QK_PAD] = k_rope
    v_ref[...] = kv[:, MLA_HEADS * MLA_NOPE:].astype(BF16)


def _mla_prep(main, tail, pos_b, freq_row, sign_row, q_norm_w, kv_norm_w, w_uq, w_ukv,
              *, cq_col, ckv_col):
    t = main.shape[0]
    rows = PREP_ROWS
    const = lambda shape: pl.BlockSpec(shape, lambda i: (0, 0))
    return pl.pallas_call(
        _mla_prep_kernel,
        grid=(t // rows,),
        in_specs=[
            pl.BlockSpec((rows, MLA_Q_RANK), lambda i: (i, cq_col // MLA_Q_RANK)),
            pl.BlockSpec((rows, MLA_KV_RANK), lambda i: (i, ckv_col // MLA_KV_RANK)),
            pl.BlockSpec((rows, LANES), lambda i: (i, 0)),
            pl.BlockSpec((rows, LANES), lambda i: (i, 0)),
            const((1, LANES)), const((1, LANES)),
            const((1, MLA_Q_RANK)), const((1, MLA_KV_RANK)),
            const(w_uq.shape), const(w_ukv.shape),
        ],
        out_specs=[
            pl.BlockSpec((rows, MLA_HEADS * MLA_QK_PAD), lambda i: (i, 0)),
            pl.BlockSpec((rows, MLA_HEADS * MLA_QK_PAD), lambda i: (i, 0)),
            pl.BlockSpec((rows, MLA_WIDTH), lambda i: (i, 0)),
        ],
        out_shape=[
            jax.ShapeDtypeStruct((t, MLA_HEADS * MLA_QK_PAD), BF16),
            jax.ShapeDtypeStruct((t, MLA_HEADS * MLA_QK_PAD), BF16),
            jax.ShapeDtypeStruct((t, MLA_WIDTH), BF16),
        ],
        compiler_params=_params("parallel"),
        name="mla_prep",
    )(main, main, tail, pos_b, freq_row, sign_row, q_norm_w.reshape(1, -1),
      kv_norm_w.reshape(1, -1), w_uq, w_ukv)


def _gla_kernel(q_ref, k_ref, v_ref, gg_ref, ga_ref, wa_ref, ba_ref, nw_ref, o_ref,
                hl_ref, qd_ref, ki_ref, ke_ref, dec_ref, att_ref, oi_ref, sall_ref, state_ref):
    seq = q_ref.shape[1]
    rows = GLA_BLOCK_ROWS
    shift = CHUNK.bit_length() - 1
    row = lax.broadcasted_iota(jnp.int32, (rows, rows), 0)
    col = lax.broadcasted_iota(jnp.int32, (rows, rows), 1)
    same_chunk = (row >> shift) == (col >> shift)
    causal = jnp.logical_and(same_chunk, row >= col)
    tri = causal.astype(BF16)
    ones = same_chunk.astype(BF16)
    nt = (((1,), (1,)), ((), ()))
    tn = (((0,), (0,)), ((), ()))

    blocks = [slice(b0, b0 + rows) for b0 in range(0, seq, rows)]
    dk = q_ref.shape[2]

    for r in blocks:
        z = jnp.dot(ga_ref[0, r, :].astype(BF16), wa_ref[...], preferred_element_type=F32)
        log_a = _log_sigmoid(z + ba_ref[...]) / GLA_TAU
        hi = log_a.astype(BF16)
        hl_ref[r, :dk] = hi
        hl_ref[r, dk:] = (log_a - hi.astype(F32)).astype(BF16)

    for r in blocks:
        hl = hl_ref[r, :]
        cum2 = jnp.dot(tri, hl, preferred_element_type=F32)
        tot2 = jnp.dot(ones, hl, preferred_element_type=F32)
        cum = cum2[:, :dk] + cum2[:, dk:]
        tot = tot2[:, :dk] + tot2[:, dk:]
        q = q_ref[0, r, :].astype(F32)
        k = k_ref[0, r, :].astype(F32)
        qd_ref[r, :] = (q * jnp.exp(cum)).astype(BF16)
        ki_ref[r, :] = (k * jnp.exp(-cum)).astype(BF16)
        ke_ref[r, :] = (k * jnp.exp(tot - cum)).astype(BF16)
        dec_ref[r, :] = jnp.exp(tot)

    for r in blocks:
        att = lax.dot_general(qd_ref[r, :], ki_ref[r, :], nt, preferred_element_type=F32)
        att_ref[r, :] = jnp.where(causal, att, 0.0).astype(BF16)

    for r in blocks:
        oi_ref[r, :] = jnp.dot(att_ref[r, :], v_ref[0, r, :], preferred_element_type=F32)

    state_ref[...] = jnp.zeros_like(state_ref)

    def body(n, carry):
        start = pl.multiple_of(n * CHUNK, CHUNK)
        r = pl.ds(start, CHUNK)
        state = state_ref[...]
        sall_ref[n] = state.astype(BF16)
        state_ref[...] = (state * dec_ref[pl.ds(start, 1), :]
                          + lax.dot_general(v_ref[0, r, :], ke_ref[r, :], tn,
                                            preferred_element_type=F32))
        return carry

    n_chunks = seq // CHUNK
    lax.fori_loop(0, n_chunks, body, 0, unroll=4)

    def inter(n):
        r = slice(n * CHUNK, (n + 1) * CHUNK)
        return lax.dot_general(qd_ref[r, :], sall_ref[n], nt, preferred_element_type=F32)

    o_inter = inter(0)
    for n in range(n_chunks):
        o_next = inter(n + 1) if n + 1 < n_chunks else None
        r = slice(n * CHUNK, (n + 1) * CHUNK)
        o = oi_ref[r, :] + o_inter
        g = gg_ref[0, r, :].astype(F32)
        o_ref[0, r, :] = (_rms(o, nw_ref[...]) * _silu(g)).astype(o_ref.dtype)
        o_inter = o_next


def _gla(main, tail, w_a2, b_a, norm_w):
    b, s, _ = main.shape
    return pl.pallas_call(
        _gla_kernel,
        grid=(b, GLA_HEADS),
        in_specs=[
            pl.BlockSpec((1, s, GLA_DK), lambda i, h: (i, 0, h)),
            pl.BlockSpec((1, s, GLA_DK), lambda i, h: (i, 0, GLA_HEADS + h)),
            pl.BlockSpec((1, s, GLA_DV), lambda i, h: (i, 0, GLA_HEADS + h)),
            pl.BlockSpec((1, s, GLA_DV), lambda i, h: (i, 0, 2 * GLA_HEADS + h)),
            pl.BlockSpec((1, s, LANES), lambda i, h: (i, 0, 0)),
            pl.BlockSpec((LANES, GLA_DK), lambda i, h: (0, h)),
            pl.BlockSpec((1, GLA_DK), lambda i, h: (0, h)),
            pl.BlockSpec((1, GLA_DV), lambda i, h: (0, 0)),
        ],
        out_specs=pl.BlockSpec((1, s, GLA_DV), lambda i, h: (i, 0, h)),
        out_shape=jax.ShapeDtypeStruct((b, s, GLA_VAL_WIDTH), BF16),
        scratch_shapes=[pltpu.VMEM((s, 2 * GLA_DK), BF16),
                        pltpu.VMEM((s, GLA_DK), BF16),
                        pltpu.VMEM((s, GLA_DK), BF16),
                        pltpu.VMEM((s, GLA_DK), BF16),
                        pltpu.VMEM((s, GLA_DK), F32),
                        pltpu.VMEM((s, GLA_BLOCK_ROWS), BF16),
                        pltpu.VMEM((s, GLA_DV), F32),
                        pltpu.VMEM((s // CHUNK, GLA_DV, GLA_DK), BF16),
                        pltpu.VMEM((GLA_DV, GLA_DK), F32)],
        compiler_params=_params("parallel", "arbitrary"),
        name="gla",
    )(main, main, main, main, tail, w_a2, b_a.reshape(1, -1), norm_w.reshape(1, -1))


def _out_proj_kernel(*refs, n_parts, final_norm):
    parts = refs[:n_parts]
    x_ref, w_ref = refs[n_parts], refs[n_parts + 1]
    if final_norm:
        fw_ref, o_ref = refs[n_parts + 2:]
    else:
        (o_ref,) = refs[n_parts + 2:]
    y = x_ref[...]
    r0 = 0
    for p in parts:
        width = p.shape[1]
        y = y + jnp.dot(p[...], w_ref[r0:r0 + width, :], preferred_element_type=F32)
        r0 += width
    if final_norm:
        y = _rms(y, fw_ref[...])
    o_ref[...] = y


def _out_proj(parts, x, w, final_w):
    t, d = x.shape
    rows = OUT_ROWS
    in_specs = [pl.BlockSpec((rows, p.shape[1]), lambda i: (i, 0)) for p in parts]
    in_specs += [pl.BlockSpec((rows, d), lambda i: (i, 0)),
                 pl.BlockSpec(w.shape, lambda i: (0, 0), pipeline_mode=pl.Buffered(1))]
    args = list(parts) + [x, w]
    if final_w is not None:
        in_specs.append(pl.BlockSpec((1, d), lambda i: (0, 0)))
        args.append(final_w.reshape(1, d))
    return pl.pallas_call(
        functools.partial(_out_proj_kernel, n_parts=len(parts), final_norm=final_w is not None),
        grid=(t // rows,),
        in_specs=in_specs,
        out_specs=pl.BlockSpec((rows, d), lambda i: (i, 0)),
        out_shape=jax.ShapeDtypeStruct((t, d), F32),
        compiler_params=_params("parallel"),
        name="out_proj",
    )(*args)


def _memory_branch(main, mem2d, mem_norm_w, w_mem_kv, *, batch, q_col, g_col):
    n = w_mem_kv.shape[1]
    mem_kv = _norm_proj(mem2d, mem_norm_w, w_mem_kv.astype(BF16), jnp.ones((1, n), F32), None,
                        rows=PROJ_ROWS, cols=n)
    mem_kv = mem_kv.reshape(batch, -1, n)
    return _attention(main, mem_kv, mem_kv, main, None, heads=MEM_HEADS, dq=MEM_HEAD_DIM,
                      dv=MEM_HEAD_DIM, q_blk=q_col // MEM_HEAD_DIM, k_blk=0, v_blk=MEM_HEADS,
                      g_blk=g_col // MEM_HEAD_DIM, mask=None, q_rows=ATTN_Q_ROWS,
                      name="mem_attn")


def _col_scale(widths_and_scales):
    return jnp.concatenate([jnp.full((1, w), s, F32) for w, s in widths_and_scales], axis=1)


def _split_cols(w, sizes):
    out, acc = [], 0
    for s in sizes:
        out.append(w[:, acc:acc + s])
        acc += s
    return out


def _even_layer(x2d, mem2d, pos_b, freq_row, sign_row, batch, norm_w, w_in, fox_b_f,
                mla_q_norm_w, mla_w_uq, mla_kv_norm_w, mla_w_ukv, mem_norm_w, w_mem_kv, w_out):
    t = x2d.shape[0]
    seq = t // batch
    (w_fq, w_fk, w_fv, w_ff, w_fg, w_cq, w_ckv, w_kr, w_mg, w_mq, w_memg) = _split_cols(
        w_in, (FOX_WIDTH, FOX_WIDTH, FOX_WIDTH, FOX_HEADS, FOX_WIDTH, MLA_Q_RANK, MLA_KV_RANK,
               MLA_ROPE, MLA_WIDTH, MEM_WIDTH, MEM_WIDTH))
    w_main = jnp.concatenate([w_fq, w_fk, w_fv, w_fg, w_cq, w_ckv, w_mg, w_mq, w_memg],
                             axis=1).astype(BF16)
    fq_col, fk_col, fv_col, fg_col = 0, FOX_WIDTH, 2 * FOX_WIDTH, 3 * FOX_WIDTH
    cq_col = 4 * FOX_WIDTH
    ckv_col = cq_col + MLA_Q_RANK
    mg_col = ckv_col + MLA_KV_RANK
    mq_col = mg_col + MLA_WIDTH
    memg_col = mq_col + MEM_WIDTH
    n_main = memg_col + MEM_WIDTH
    ff_lane = MLA_ROPE
    w_tail = jnp.concatenate(
        [w_kr, w_ff, jnp.zeros((D_MODEL, LANES - MLA_ROPE - FOX_HEADS), F32)], axis=1).astype(BF16)
    col_scale = _col_scale([(FOX_WIDTH, FOX_HEAD_DIM ** -0.5 * LOG2E), (mq_col - FOX_WIDTH, 1.0),
                            (MEM_WIDTH, MEM_HEAD_DIM ** -0.5 * LOG2E), (MEM_WIDTH, 1.0)])
    main, tail = _norm_proj(x2d, norm_w, w_main, col_scale, w_tail, rows=PROJ_ROWS, cols=768)
    main3 = main.reshape(batch, seq, n_main)

    bias_row = jnp.zeros((1, LANES), F32).at[0, ff_lane:ff_lane + FOX_HEADS].set(fox_b_f)
    neg_c = _fox_gate(tail.reshape(batch, seq, LANES), bias_row, lane0=ff_lane, heads=FOX_HEADS)
    o_a = _attention(main3, main3, main3, main3, neg_c.reshape(batch * FOX_HEADS, 1, seq),
                     heads=FOX_HEADS, dq=FOX_HEAD_DIM, dv=FOX_HEAD_DIM,
                     q_blk=fq_col // FOX_HEAD_DIM, k_blk=fk_col // FOX_HEAD_DIM,
                     v_blk=fv_col // FOX_HEAD_DIM, g_blk=fg_col // FOX_HEAD_DIM,
                     mask="causal", q_rows=ATTN_Q_ROWS, name="fox_attn")

    pad = jnp.zeros((MLA_Q_RANK, MLA_HEADS, MLA_QK_PAD - MLA_NOPE - MLA_ROPE), F32)
    w_uq = jnp.concatenate([mla_w_uq.reshape(MLA_Q_RANK, MLA_HEADS, MLA_NOPE + MLA_ROPE), pad],
                           axis=2).reshape(MLA_Q_RANK, MLA_HEADS * MLA_QK_PAD).astype(BF16)
    w_ukv = mla_w_ukv.reshape(MLA_KV_RANK, MLA_HEADS, MLA_NOPE + MLA_V)
    w_ukv = jnp.concatenate([w_ukv[:, :, :MLA_NOPE].reshape(MLA_KV_RANK, -1),
                             w_ukv[:, :, MLA_NOPE:].reshape(MLA_KV_RANK, -1)], axis=1).astype(BF16)
    q_cat, k_cat, v_b = _mla_prep(main, tail, pos_b, freq_row, sign_row, mla_q_norm_w,
                                  mla_kv_norm_w, w_uq, w_ukv, cq_col=cq_col, ckv_col=ckv_col)
    o_b = _attention(q_cat.reshape(batch, seq, -1), k_cat.reshape(batch, seq, -1),
                     v_b.reshape(batch, seq, -1), main3, None, heads=MLA_HEADS, dq=MLA_QK_PAD,
                     dv=MLA_V, q_blk=0, k_blk=0, v_blk=0, g_blk=mg_col // MLA_V,
                     mask="chunk", q_rows=ATTN_Q_ROWS, name="mla_attn")

    o_m = _memory_branch(main3, mem2d, mem_norm_w, w_mem_kv, batch=batch, q_col=mq_col,
                         g_col=memg_col)
    parts = [o_a.reshape(t, -1), o_b.reshape(t, -1), o_m.reshape(t, -1)]
    return _out_proj(parts, x2d, w_out.astype(BF16), None)


def _odd_layer(x2d, mem2d, batch, norm_w, w_in, gla_w_a2, gla_b_a, gla_norm_w, mem_norm_w,
               w_mem_kv, w_out, final_w):
    t = x2d.shape[0]
    seq = t // batch
    (w_gq, w_gk, w_gv, w_ga, w_gg, w_mq, w_memg) = _split_cols(
        w_in, (GLA_KEY_WIDTH, GLA_KEY_WIDTH, GLA_VAL_WIDTH, GLA_GATE_RANK, GLA_VAL_WIDTH,
               MEM_WIDTH, MEM_WIDTH))
    w_main = jnp.concatenate([w_gq, w_gk, w_gv, w_gg, w_mq, w_memg], axis=1).astype(BF16)
    mq_col = 2 * GLA_KEY_WIDTH + 2 * GLA_VAL_WIDTH
    memg_col = mq_col + MEM_WIDTH
    n_main = memg_col + MEM_WIDTH
    w_tail = jnp.concatenate([w_ga, jnp.zeros((D_MODEL, LANES - GLA_GATE_RANK), F32)],
                             axis=1).astype(BF16)
    col_scale = _col_scale([(GLA_KEY_WIDTH, GLA_DK ** -0.5), (mq_col - GLA_KEY_WIDTH, 1.0),
                            (MEM_WIDTH, MEM_HEAD_DIM ** -0.5 * LOG2E), (MEM_WIDTH, 1.0)])
    main, tail = _norm_proj(x2d, norm_w, w_main, col_scale, w_tail, rows=PROJ_ROWS, cols=1024)
    main3 = main.reshape(batch, seq, n_main)

    w_a2 = jnp.concatenate([gla_w_a2, jnp.zeros((LANES - GLA_GATE_RANK, GLA_KEY_WIDTH), F32)],
                           axis=0).astype(BF16)
    o_c = _gla(main3, tail.reshape(batch, seq, LANES), w_a2, gla_b_a, gla_norm_w)
    o_m = _memory_branch(main3, mem2d, mem_norm_w, w_mem_kv, batch=batch, q_col=mq_col,
                         g_col=memg_col)
    parts = [o_c.reshape(t, -1), o_m.reshape(t, -1)]
    return _out_proj(parts, x2d, w_out.astype(BF16), final_w)


def kernel(x, mem, positions, e_norm_w, e_w_in, e_fox_b_f, e_mla_q_norm_w, e_mla_w_uq, e_mla_kv_norm_w, e_mla_w_ukv, e_mem_norm_w, e_w_mem_kv, e_w_out, o_norm_w, o_w_in, o_gla_w_a2, o_gla_b_a, o_gla_norm_w, o_mem_norm_w, o_w_mem_kv, o_w_out, final_norm_w):
    batch, seq, d = x.shape
    t = batch * seq
    x2d = x.reshape(t, d)
    mem2d = mem.reshape(-1, d)
    inv_freq = 1.0 / (ROPE_THETA ** (jnp.arange(0, MLA_ROPE, 2, dtype=F32) / MLA_ROPE))
    half = MLA_ROPE // 2
    freq_row = jnp.zeros((1, LANES), F32).at[0, :half].set(inv_freq).at[0, half:2 * half].set(inv_freq)
    sign_row = jnp.zeros((1, LANES), F32).at[0, :half].set(-1.0).at[0, half:2 * half].set(1.0)
    pos_b = jnp.broadcast_to(positions.astype(F32).reshape(t, 1), (t, LANES))

    x2d = _even_layer(x2d, mem2d, pos_b, freq_row, sign_row, batch, e_norm_w[0], e_w_in[0],
                      e_fox_b_f[0], e_mla_q_norm_w[0], e_mla_w_uq[0], e_mla_kv_norm_w[0],
                      e_mla_w_ukv[0], e_mem_norm_w[0], e_w_mem_kv[0], e_w_out[0])
    x2d = _odd_layer(x2d, mem2d, batch, o_norm_w[0], o_w_in[0], o_gla_w_a2[0], o_gla_b_a[0],
                     o_gla_norm_w[0], o_mem_norm_w[0], o_w_mem_kv[0], o_w_out[0], final_norm_w)
    return x2d.reshape(batch, seq, d)
```

```python
import functools

import jax
import jax.numpy as jnp
from jax import lax
from jax.experimental import pallas as pl
from jax.experimental.pallas import tpu as pltpu

F32 = jnp.float32
BF16 = jnp.bfloat16

D_MODEL = 2048
CHUNK = 64
EPS = 1e-6
NEG = -1e30

FOX_HEADS = 8
FOX_HEAD_DIM = 128
FOX_WIDTH = FOX_HEADS * FOX_HEAD_DIM

MLA_HEADS = 8
MLA_NOPE = 128
MLA_ROPE = 64
MLA_V = 128
MLA_Q_RANK = 512
MLA_KV_RANK = 256
MLA_WIDTH = MLA_HEADS * MLA_V
MLA_QK_PAD = 256
ROPE_THETA = 10000.0

GLA_HEADS = 4
GLA_KEY_WIDTH = D_MODEL // 2
GLA_VAL_WIDTH = D_MODEL
GLA_DK = GLA_KEY_WIDTH // GLA_HEADS
GLA_DV = GLA_VAL_WIDTH // GLA_HEADS
GLA_GATE_RANK = 16
GLA_TAU = 16.0

MEM_HEADS = 4
MEM_HEAD_DIM = 128
MEM_WIDTH = MEM_HEADS * MEM_HEAD_DIM

LANES = 128
VMEM_LIMIT_BYTES = 56 * 1024 * 1024
PROJ_ROWS = 256
PROJ_COLS = 256
OUT_ROWS = 512
PREP_ROWS = 512
ATTN_Q_ROWS = 256
ATTN_K_COLS = 256
LOG2E = 1.4426950408889634
GLA_BLOCK_ROWS = 256


def _params(*sem):
    return pltpu.CompilerParams(dimension_semantics=sem, vmem_limit_bytes=VMEM_LIMIT_BYTES)


def _rms(x, w):
    return x * lax.rsqrt(jnp.mean(x * x, axis=-1, keepdims=True) + EPS) * w


def _log_sigmoid(z):
    return jnp.minimum(z, 0.0) - jnp.log(1.0 + jnp.exp(-jnp.abs(z)))


def _silu(g):
    return g * jax.nn.sigmoid(g)


def _norm_proj_kernel(x_ref, *rest, n_groups, with_norm, with_tail):
    rest = list(rest)
    nw_ref = rest.pop(0) if with_norm else None
    w_refs = [rest.pop(0) for _ in range(n_groups)]
    cs_ref = rest.pop(0)
    wt_ref = rest.pop(0) if with_tail else None
    o_ref = rest.pop(0)
    ot_ref = rest.pop(0) if with_tail else None
    if with_norm:
        (h_ref,) = rest
        h_ref[...] = _rms(x_ref[...], nw_ref[...]).astype(BF16)
    else:
        h_ref = x_ref
    if with_tail:
        ot_ref[...] = jnp.dot(h_ref[...], wt_ref[...], preferred_element_type=F32)
    c0 = 0
    for w_ref in w_refs:
        for c in range(0, w_ref.shape[1], PROJ_COLS):
            acc = jnp.dot(h_ref[...], w_ref[:, c:c + PROJ_COLS], preferred_element_type=F32)
            dst = slice(c0 + c, c0 + c + PROJ_COLS)
            o_ref[:, dst] = (acc * cs_ref[:, dst]).astype(o_ref.dtype)
        c0 += w_ref.shape[1]


def _norm_proj(x, norm_w, ws, col_scale, w_tail, *, rows):
    t, k = x.shape
    n = sum(w.shape[1] for w in ws)
    with_norm = norm_w is not None
    with_tail = w_tail is not None

    def resident(shape):
        return pl.BlockSpec(shape, lambda i: (0, 0), pipeline_mode=pl.Buffered(1))

    in_specs = [pl.BlockSpec((rows, k), lambda i: (i, 0))]
    args = [x]
    if with_norm:
        in_specs.append(resident((1, k)))
        args.append(norm_w.reshape(1, k))
    in_specs += [resident(w.shape) for w in ws] + [resident((1, n))]
    args += list(ws) + [col_scale]
    out_shape = [jax.ShapeDtypeStruct((t, n), BF16)]
    out_specs = [pl.BlockSpec((rows, n), lambda i: (i, 0))]
    if with_tail:
        in_specs.append(resident((k, LANES)))
        args.append(w_tail)
        out_shape.append(jax.ShapeDtypeStruct((t, LANES), F32))
        out_specs.append(pl.BlockSpec((rows, LANES), lambda i: (i, 0)))
    outs = pl.pallas_call(
        functools.partial(_norm_proj_kernel, n_groups=len(ws), with_norm=with_norm,
                          with_tail=with_tail),
        grid=(t // rows,),
        in_specs=in_specs,
        out_specs=out_specs,
        out_shape=out_shape,
        scratch_shapes=[pltpu.VMEM((rows, k), BF16)] if with_norm else [],
        compiler_params=_params("parallel"),
        name="norm_proj",
    )(*args)
    return outs if with_tail else outs[0]


def _fox_gate_kernel(tail_ref, bias_ref, o_ref, *, lane0, heads):
    seq = tail_ref.shape[1]
    logf = _log_sigmoid(tail_ref[0] + bias_ref[...])
    c = logf.T[lane0:lane0 + heads, :]
    lane = lax.broadcasted_iota(jnp.int32, c.shape, 1)
    shift = 1
    while shift < seq:
        c = c + jnp.where(lane >= shift, pltpu.roll(c, shift, axis=1), 0.0)
        shift *= 2
    o_ref[0] = c * -LOG2E


def _fox_gate(tail, bias_row, *, lane0, heads):
    b, s, _ = tail.shape
    return pl.pallas_call(
        functools.partial(_fox_gate_kernel, lane0=lane0, heads=heads),
        grid=(b,),
        in_specs=[pl.BlockSpec((1, s, LANES), lambda i: (i, 0, 0)),
                  pl.BlockSpec((1, LANES), lambda i: (0, 0))],
        out_specs=pl.BlockSpec((1, heads, s), lambda i: (i, 0, 0)),
        out_shape=jax.ShapeDtypeStruct((b, heads, s), F32),
        compiler_params=_params("parallel"),
        name="fox_gate",
    )(tail, bias_row)


def _attn_kernel(q_ref, k_ref, v_ref, g_ref, *rest, mask, q_rows, with_bias):
    if with_bias:
        b_ref, o_ref, s_ref, v1_ref = rest
    else:
        o_ref, s_ref, v1_ref = rest
    seq, sk = q_ref.shape[1], k_ref.shape[1]
    dv = v_ref.shape[2]
    v1_ref[:, :dv] = v_ref[0]
    v1_ref[:, dv:] = jnp.ones((sk, dv), BF16)
    k_cols = ATTN_K_COLS
    nt = (((1,), (1,)), ((), ()))

    def lane_slabs(t):
        return [t[:, c:c + LANES] for c in range(0, t.shape[1], LANES)]

    def tiles(i):
        return range(0, sk if mask is None else (i + 1) * q_rows, k_cols)

    def pass1(i):
        lo = i * q_rows
        q = q_ref[0, lo:lo + q_rows, :]
        mx = jnp.full((q_rows, LANES), NEG, F32)
        for a in tiles(i):
            s = lax.dot_general(q, k_ref[0, a:a + k_cols, :], nt, preferred_element_type=F32)
            if with_bias:
                s = s + b_ref[0, :, a:a + k_cols]
            if mask is not None and a + k_cols > lo + 1:
                row = lo + lax.broadcasted_iota(jnp.int32, s.shape, 0)
                col = a + lax.broadcasted_iota(jnp.int32, s.shape, 1)
                if mask == "chunk":
                    shift = CHUNK.bit_length() - 1
                    row, col = row >> shift, col >> shift
                s = jnp.where(row >= col, s, NEG)
            s_ref[i % 2, :, a:a + k_cols] = s
            mx = functools.reduce(jnp.maximum, lane_slabs(s), mx)
        return jnp.max(mx, axis=-1, keepdims=True)

    def pass2(i, m):
        lo = i * q_rows
        o = jnp.zeros((q_rows, 2 * dv), F32)
        for a in tiles(i):
            p = jnp.exp2(s_ref[i % 2, :, a:a + k_cols] - m)
            o = o + jnp.dot(p.astype(BF16), v1_ref[a:a + k_cols, :],
                            preferred_element_type=F32)
        g = g_ref[0, lo:lo + q_rows, :].astype(F32)
        o_ref[0, lo:lo + q_rows, :] = (o[:, :dv] / o[:, dv:] * _silu(g)).astype(o_ref.dtype)

    n_blocks = seq // q_rows
    m = pass1(0)
    for i in range(n_blocks):
        m_next = pass1(i + 1) if i + 1 < n_blocks else None
        pass2(i, m)
        m = m_next


def _attention(q, k, v, gate, bias, *, heads, dq, dv, q_blk, k_blk, v_blk, g_blk,
               mask, q_rows, name):
    b, s, _ = q.shape
    sk = k.shape[1]
    in_specs = [
        pl.BlockSpec((1, s, dq), lambda i, h: (i, 0, q_blk + h)),
        pl.BlockSpec((1, sk, dq), lambda i, h: (i, 0, k_blk + h)),
        pl.BlockSpec((1, sk, dv), lambda i, h: (i, 0, v_blk + h)),
        pl.BlockSpec((1, s, dv), lambda i, h: (i, 0, g_blk + h)),
    ]
    args = [q, k, v, gate]
    if bias is not None:
        in_specs.append(pl.BlockSpec((1, 1, sk), lambda i, h: (i * heads + h, 0, 0)))
        args.append(bias)
    return pl.pallas_call(
        functools.partial(_attn_kernel, mask=mask, q_rows=q_rows, with_bias=bias is not None),
        grid=(b, heads),
        in_specs=in_specs,
        out_specs=pl.BlockSpec((1, s, dv), lambda i, h: (i, 0, h)),
        out_shape=jax.ShapeDtypeStruct((b, s, heads * dv), BF16),
        scratch_shapes=[pltpu.VMEM((2, q_rows, sk), F32), pltpu.VMEM((sk, 2 * dv), BF16)],
        compiler_params=_params("parallel", "arbitrary"),
        name=name,
    )(*args)


def _rope(v, cos, sin_signed, lane):
    swapped = jnp.where(lane < MLA_ROPE // 2,
                        pltpu.roll(v, LANES - MLA_ROPE // 2, axis=1),
                        pltpu.roll(v, MLA_ROPE // 2, axis=1))
    return v * cos + swapped * sin_signed


def _mla_prep_kernel(cq_ref, ckv_ref, tail_ref, pos_ref, freq_ref, sign_ref, qnw_ref, kvnw_ref,
                     wuq_ref, wukv_ref, q_ref, k_ref, v_ref):
    rows = cq_ref.shape[0]
    ang = pos_ref[...] * freq_ref[...]
    cos = jnp.cos(ang)
    sin_signed = jnp.sin(ang) * sign_ref[...]
    lane = lax.broadcasted_iota(jnp.int32, (rows, LANES), 1)
    scale = (MLA_NOPE + MLA_ROPE) ** -0.5 * LOG2E

    cqn = _rms(cq_ref[...].astype(F32), qnw_ref[...]).astype(BF16)
    q = jnp.dot(cqn, wuq_ref[...], preferred_element_type=F32) * scale
    ckvn = _rms(ckv_ref[...].astype(F32), kvnw_ref[...]).astype(BF16)
    kv = jnp.dot(ckvn, wukv_ref[...], preferred_element_type=F32)
    k_rope = jnp.where(lane < MLA_ROPE, _rope(tail_ref[...], cos, sin_signed, lane), 0.0)
    k_rope = k_rope.astype(BF16)
    for h in range(MLA_HEADS):
        c0 = h * MLA_QK_PAD
        q_ref[:, c0:c0 + MLA_NOPE] = q[:, c0:c0 + MLA_NOPE].astype(BF16)
        q_ref[:, c0 + MLA_NOPE:c0 + MLA_QK_PAD] = _rope(
            q[:, c0 + MLA_NOPE:c0 + MLA_QK_PAD], cos, sin_signed, lane).astype(BF16)
        k_ref[:, c0:c0 + MLA_NOPE] = kv[:, h * MLA_NOPE:(h + 1) * MLA_NOPE].astype(BF16)
        k_ref[:, c0 + MLA_NOPE:c0 + MLA_QK_PAD] = k_rope
    v_ref[...] = kv[:, MLA_HEADS * MLA_NOPE:].astype(BF16)


def _mla_prep(main, tail, pos_b, freq_row, sign_row, q_norm_w, kv_norm_w, w_uq, w_ukv,
              *, cq_col, ckv_col):
    t = main.shape[0]
    rows = PREP_ROWS
    const = lambda shape: pl.BlockSpec(shape, lambda i: (0, 0))
    return pl.pallas_call(
        _mla_prep_kernel,
        grid=(t // rows,),
        in_specs=[
            pl.BlockSpec((rows, MLA_Q_RANK), lambda i: (i, cq_col // MLA_Q_RANK)),
            pl.BlockSpec((rows, MLA_KV_RANK), lambda i: (i, ckv_col // MLA_KV_RANK)),
            pl.BlockSpec((rows, LANES), lambda i: (i, 0)),
            pl.BlockSpec((rows, LANES), lambda i: (i, 0)),
            const((1, LANES)), const((1, LANES)),
            const((1, MLA_Q_RANK)), const((1, MLA_KV_RANK)),
            const(w_uq.shape), const(w_ukv.shape),
        ],
        out_specs=[
            pl.BlockSpec((rows, MLA_HEADS * MLA_QK_PAD), lambda i: (i, 0)),
            pl.BlockSpec((rows, MLA_HEADS * MLA_QK_PAD), lambda i: (i, 0)),
            pl.BlockSpec((rows, MLA_WIDTH), lambda i: (i, 0)),
        ],
        out_shape=[
            jax.ShapeDtypeStruct((t, MLA_HEADS * MLA_QK_PAD), BF16),
            jax.ShapeDtypeStruct((t, MLA_HEADS * MLA_QK_PAD), BF16),
            jax.ShapeDtypeStruct((t, MLA_WIDTH), BF16),
        ],
        compiler_params=_params("parallel"),
        name="mla_prep",
    )(main, main, tail, pos_b, freq_row, sign_row, q_norm_w.reshape(1, -1),
      kv_norm_w.reshape(1, -1), w_uq, w_ukv)


def _gla_kernel(q_ref, k_ref, v_ref, gg_ref, ga_ref, wa_ref, ba_ref, nw_ref, o_ref,
                hl_ref, qd_ref, ki_ref, ke_ref, dec_ref, att_ref, oi_ref, sall_ref, state_ref):
    seq = q_ref.shape[1]
    rows = GLA_BLOCK_ROWS
    shift = CHUNK.bit_length() - 1
    row = lax.broadcasted_iota(jnp.int32, (rows, rows), 0)
    col = lax.broadcasted_iota(jnp.int32, (rows, rows), 1)
    same_chunk = (row >> shift) == (col >> shift)
    causal = jnp.logical_and(same_chunk, row >= col)
    tri = causal.astype(BF16)
    ones = same_chunk.astype(BF16)
    nt = (((1,), (1,)), ((), ()))
    tn = (((0,), (0,)), ((), ()))

    blocks = [slice(b0, b0 + rows) for b0 in range(0, seq, rows)]
    dk = q_ref.shape[2]

    for r in blocks:
        z = jnp.dot(ga_ref[0, r, :].astype(BF16), wa_ref[...], preferred_element_type=F32)
        log_a = _log_sigmoid(z + ba_ref[...]) / GLA_TAU
        hi = log_a.astype(BF16)
        hl_ref[r, :dk] = hi
        hl_ref[r, dk:] = (log_a - hi.astype(F32)).astype(BF16)

    for r in blocks:
        hl = hl_ref[r, :]
        cum2 = jnp.dot(tri, hl, preferred_element_type=F32)
        tot2 = jnp.dot(ones, hl, preferred_element_type=F32)
        cum = cum2[:, :dk] + cum2[:, dk:]
        tot = tot2[:, :dk] + tot2[:, dk:]
        q = q_ref[0, r, :].astype(F32)
        k = k_ref[0, r, :].astype(F32)
        qd_ref[r, :] = (q * jnp.exp(cum)).astype(BF16)
        ki_ref[r, :] = (k * jnp.exp(-cum)).astype(BF16)
        ke_ref[r, :] = (k * jnp.exp(tot - cum)).astype(BF16)
        dec_ref[r, :] = jnp.exp(tot)

    for r in blocks:
        att = lax.dot_general(qd_ref[r, :], ki_ref[r, :], nt, preferred_element_type=F32)
        att_ref[r, :] = jnp.where(causal, att, 0.0).astype(BF16)

    for r in blocks:
        oi_ref[r, :] = jnp.dot(att_ref[r, :], v_ref[0, r, :], preferred_element_type=F32)

    state_ref[...] = jnp.zeros_like(state_ref)

    def body(n, carry):
        start = pl.multiple_of(n * CHUNK, CHUNK)
        r = pl.ds(start, CHUNK)
        state = state_ref[...]
        sall_ref[n] = state.astype(BF16)
        state_ref[...] = (state * dec_ref[pl.ds(start, 1), :]
                          + lax.dot_general(v_ref[0, r, :], ke_ref[r, :], tn,
                                            preferred_element_type=F32))
        return carry

    n_chunks = seq // CHUNK
    lax.fori_loop(0, n_chunks, body, 0, unroll=4)

    def inter(n):
        r = slice(n * CHUNK, (n + 1) * CHUNK)
        return lax.dot_general(qd_ref[r, :], sall_ref[n], nt, preferred_element_type=F32)

    o_inter = inter(0)
    for n in range(n_chunks):
        o_next = inter(n + 1) if n + 1 < n_chunks else None
        r = slice(n * CHUNK, (n + 1) * CHUNK)
        o = oi_ref[r, :] + o_inter
        g = gg_ref[0, r, :].astype(F32)
        o_ref[0, r, :] = (_rms(o, nw_ref[...]) * _silu(g)).astype(o_ref.dtype)
        o_inter = o_next


def _gla(main, tail, w_a2, b_a, norm_w):
    b, s, _ = main.shape
    return pl.pallas_call(
        _gla_kernel,
        grid=(b, GLA_HEADS),
        in_specs=[
            pl.BlockSpec((1, s, GLA_DK), lambda i, h: (i, 0, h)),
            pl.BlockSpec((1, s, GLA_DK), lambda i, h: (i, 0, GLA_HEADS + h)),
            pl.BlockSpec((1, s, GLA_DV), lambda i, h: (i, 0, GLA_HEADS + h)),
            pl.BlockSpec((1, s, GLA_DV), lambda i, h: (i, 0, 2 * GLA_HEADS + h)),
            pl.BlockSpec((1, s, LANES), lambda i, h: (i, 0, 0)),
            pl.BlockSpec((LANES, GLA_DK), lambda i, h: (0, h)),
            pl.BlockSpec((1, GLA_DK), lambda i, h: (0, h)),
            pl.BlockSpec((1, GLA_DV), lambda i, h: (0, 0)),
        ],
        out_specs=pl.BlockSpec((1, s, GLA_DV), lambda i, h: (i, 0, h)),
        out_shape=jax.ShapeDtypeStruct((b, s, GLA_VAL_WIDTH), BF16),
        scratch_shapes=[pltpu.VMEM((s, 2 * GLA_DK), BF16),
                        pltpu.VMEM((s, GLA_DK), BF16),
                        pltpu.VMEM((s, GLA_DK), BF16),
                        pltpu.VMEM((s, GLA_DK), BF16),
                        pltpu.VMEM((s, GLA_DK), F32),
                        pltpu.VMEM((s, GLA_BLOCK_ROWS), BF16),
                        pltpu.VMEM((s, GLA_DV), F32),
                        pltpu.VMEM((s // CHUNK, GLA_DV, GLA_DK), BF16),
                        pltpu.VMEM((GLA_DV, GLA_DK), F32)],
        compiler_params=_params("parallel", "arbitrary"),
        name="gla",
    )(main, main, main, main, tail, w_a2, b_a.reshape(1, -1), norm_w.reshape(1, -1))


def _out_proj_kernel(*refs, n_parts, keep_residual):
    parts = refs[:n_parts]
    x_ref, w_ref, nw_ref = refs[n_parts:n_parts + 3]
    outs = refs[n_parts + 3:]
    y = x_ref[...]
    r0 = 0
    for p in parts:
        width = p.shape[1]
        y = y + jnp.dot(p[...], w_ref[r0:r0 + width, :], preferred_element_type=F32)
        r0 += width
    if keep_residual:
        y_ref, h_ref = outs
        y_ref[...] = y
    else:
        (h_ref,) = outs
    h_ref[...] = _rms(y, nw_ref[...]).astype(h_ref.dtype)


def _out_proj(parts, x, w, norm_w, *, keep_residual):
    t, d = x.shape
    rows = OUT_ROWS
    in_specs = [pl.BlockSpec((rows, p.shape[1]), lambda i: (i, 0)) for p in parts]
    in_specs += [pl.BlockSpec((rows, d), lambda i: (i, 0)),
                 pl.BlockSpec(w.shape, lambda i: (0, 0), pipeline_mode=pl.Buffered(1)),
                 pl.BlockSpec((1, d), lambda i: (0, 0))]
    args = list(parts) + [x, w, norm_w.reshape(1, d)]
    row_spec = pl.BlockSpec((rows, d), lambda i: (i, 0))
    if keep_residual:
        out_specs = [row_spec, row_spec]
        out_shape = [jax.ShapeDtypeStruct((t, d), F32), jax.ShapeDtypeStruct((t, d), BF16)]
    else:
        out_specs = row_spec
        out_shape = jax.ShapeDtypeStruct((t, d), F32)
    return pl.pallas_call(
        functools.partial(_out_proj_kernel, n_parts=len(parts), keep_residual=keep_residual),
        grid=(t // rows,),
        in_specs=in_specs,
        out_specs=out_specs,
        out_shape=out_shape,
        compiler_params=_params("parallel"),
        name="out_proj",
    )(*args)


def _memory_branch(main, mem2d, mem_norm_w, w_mem_kv, *, batch, q_col, g_col):
    n = w_mem_kv.shape[1]
    mem_kv = _norm_proj(mem2d, mem_norm_w, [w_mem_kv.astype(BF16)], jnp.ones((1, n), F32), None,
                        rows=PROJ_ROWS)
    mem_kv = mem_kv.reshape(batch, -1, n)
    return _attention(main, mem_kv, mem_kv, main, None, heads=MEM_HEADS, dq=MEM_HEAD_DIM,
                      dv=MEM_HEAD_DIM, q_blk=q_col // MEM_HEAD_DIM, k_blk=0, v_blk=MEM_HEADS,
                      g_blk=g_col // MEM_HEAD_DIM, mask=None, q_rows=ATTN_Q_ROWS,
                      name="mem_attn")


def _col_scale(widths_and_scales):
    return jnp.concatenate([jnp.full((1, w), s, F32) for w, s in widths_and_scales], axis=1)


def _split_cols(w, sizes):
    out, acc = [], 0
    for s in sizes:
        out.append(w[:, acc:acc + s])
        acc += s
    return out


def _even_layer(x2d, mem2d, pos_b, freq_row, sign_row, batch, norm_w, w_in, fox_b_f,
                mla_q_norm_w, mla_w_uq, mla_kv_norm_w, mla_w_ukv, mem_norm_w, w_mem_kv, w_out,
                next_norm_w):
    t = x2d.shape[0]
    seq = t // batch
    (w_a, w_ff, w_b, w_kr, w_c) = _split_cols(
        w_in.astype(BF16), (3 * FOX_WIDTH, FOX_HEADS, FOX_WIDTH + MLA_Q_RANK + MLA_KV_RANK,
                            MLA_ROPE, MLA_WIDTH + 2 * MEM_WIDTH))
    fq_col, fk_col, fv_col, fg_col = 0, FOX_WIDTH, 2 * FOX_WIDTH, 3 * FOX_WIDTH
    cq_col = 4 * FOX_WIDTH
    ckv_col = cq_col + MLA_Q_RANK
    mg_col = ckv_col + MLA_KV_RANK
    mq_col = mg_col + MLA_WIDTH
    memg_col = mq_col + MEM_WIDTH
    n_main = memg_col + MEM_WIDTH
    ff_lane = MLA_ROPE
    w_tail = jnp.concatenate(
        [w_kr, w_ff, jnp.zeros((D_MODEL, LANES - MLA_ROPE - FOX_HEADS), BF16)], axis=1)
    col_scale = _col_scale([(FOX_WIDTH, FOX_HEAD_DIM ** -0.5 * LOG2E), (mq_col - FOX_WIDTH, 1.0),
                            (MEM_WIDTH, MEM_HEAD_DIM ** -0.5 * LOG2E), (MEM_WIDTH, 1.0)])
    main, tail = _norm_proj(x2d, norm_w, [w_a, w_b, w_c], col_scale, w_tail, rows=PROJ_ROWS)
    main3 = main.reshape(batch, seq, n_main)

    bias_row = jnp.zeros((1, LANES), F32).at[0, ff_lane:ff_lane + FOX_HEADS].set(fox_b_f)
    neg_c = _fox_gate(tail.reshape(batch, seq, LANES), bias_row, lane0=ff_lane, heads=FOX_HEADS)
    o_a = _attention(main3, main3, main3, main3, neg_c.reshape(batch * FOX_HEADS, 1, seq),
                     heads=FOX_HEADS, dq=FOX_HEAD_DIM, dv=FOX_HEAD_DIM,
                     q_blk=fq_col // FOX_HEAD_DIM, k_blk=fk_col // FOX_HEAD_DIM,
                     v_blk=fv_col // FOX_HEAD_DIM, g_blk=fg_col // FOX_HEAD_DIM,
                     mask="causal", q_rows=ATTN_Q_ROWS, name="fox_attn")

    pad = jnp.zeros((MLA_Q_RANK, MLA_HEADS, MLA_QK_PAD - MLA_NOPE - MLA_ROPE), F32)
    w_uq = jnp.concatenate([mla_w_uq.reshape(MLA_Q_RANK, MLA_HEADS, MLA_NOPE + MLA_ROPE), pad],
                           axis=2).reshape(MLA_Q_RANK, MLA_HEADS * MLA_QK_PAD).astype(BF16)
    w_ukv = mla_w_ukv.reshape(MLA_KV_RANK, MLA_HEADS, MLA_NOPE + MLA_V)
    w_ukv = jnp.concatenate([w_ukv[:, :, :MLA_NOPE].reshape(MLA_KV_RANK, -1),
                             w_ukv[:, :, MLA_NOPE:].reshape(MLA_KV_RANK, -1)], axis=1).astype(BF16)
    q_cat, k_cat, v_b = _mla_prep(main, tail, pos_b, freq_row, sign_row, mla_q_norm_w,
                                  mla_kv_norm_w, w_uq, w_ukv, cq_col=cq_col, ckv_col=ckv_col)
    o_b = _attention(q_cat.reshape(batch, seq, -1), k_cat.reshape(batch, seq, -1),
                     v_b.reshape(batch, seq, -1), main3, None, heads=MLA_HEADS, dq=MLA_QK_PAD,
                     dv=MLA_V, q_blk=0, k_blk=0, v_blk=0, g_blk=mg_col // MLA_V,
                     mask="chunk", q_rows=ATTN_Q_ROWS, name="mla_attn")

    o_m = _memory_branch(main3, mem2d, mem_norm_w, w_mem_kv, batch=batch, q_col=mq_col,
                         g_col=memg_col)
    parts = [o_a.reshape(t, -1), o_b.reshape(t, -1), o_m.reshape(t, -1)]
    return _out_proj(parts, x2d, w_out.astype(BF16), next_norm_w, keep_residual=True)


def _odd_layer(x2d, h2d, mem2d, batch, w_in, gla_w_a2, gla_b_a, gla_norm_w, mem_norm_w,
               w_mem_kv, w_out, final_w):
    t = x2d.shape[0]
    seq = t // batch
    (w_a, w_ga, w_b) = _split_cols(
        w_in.astype(BF16), (2 * GLA_KEY_WIDTH + GLA_VAL_WIDTH, GLA_GATE_RANK,
                            GLA_VAL_WIDTH + 2 * MEM_WIDTH))
    mq_col = 2 * GLA_KEY_WIDTH + 2 * GLA_VAL_WIDTH
    memg_col = mq_col + MEM_WIDTH
    n_main = memg_col + MEM_WIDTH
    w_tail = jnp.concatenate([w_ga, jnp.zeros((D_MODEL, LANES - GLA_GATE_RANK), BF16)], axis=1)
    col_scale = _col_scale([(GLA_KEY_WIDTH, GLA_DK ** -0.5), (mq_col - GLA_KEY_WIDTH, 1.0),
                            (MEM_WIDTH, MEM_HEAD_DIM ** -0.5 * LOG2E), (MEM_WIDTH, 1.0)])
    main, tail = _norm_proj(h2d, None, [w_a, w_b], col_scale, w_tail, rows=PROJ_ROWS)
    main3 = main.reshape(batch, seq, n_main)

    w_a2 = jnp.concatenate([gla_w_a2, jnp.zeros((LANES - GLA_GATE_RANK, GLA_KEY_WIDTH), F32)],
                           axis=0).astype(BF16)
    o_c = _gla(main3, tail.reshape(batch, seq, LANES), w_a2, gla_b_a, gla_norm_w)
    o_m = _memory_branch(main3, mem2d, mem_norm_w, w_mem_kv, batch=batch, q_col=mq_col,
                         g_col=memg_col)
    parts = [o_c.reshape(t, -1), o_m.reshape(t, -1)]
    return _out_proj(parts, x2d, w_out.astype(BF16), final_w, keep_residual=False)


def kernel(x, mem, positions, e_norm_w, e_w_in, e_fox_b_f, e_mla_q_norm_w, e_mla_w_uq, e_mla_kv_norm_w, e_mla_w_ukv, e_mem_norm_w, e_w_mem_kv, e_w_out, o_norm_w, o_w_in, o_gla_w_a2, o_gla_b_a, o_gla_norm_w, o_mem_norm_w, o_w_mem_kv, o_w_out, final_norm_w):
    batch, seq, d = x.shape
    t = batch * seq
    x2d = x.reshape(t, d)
    mem2d = mem.reshape(-1, d)
    inv_freq = 1.0 / (ROPE_THETA ** (jnp.arange(0, MLA_ROPE, 2, dtype=F32) / MLA_ROPE))
    half = MLA_ROPE // 2
    freq_row = jnp.zeros((1, LANES), F32).at[0, :half].set(inv_freq).at[0, half:2 * half].set(inv_freq)
    sign_row = jnp.zeros((1, LANES), F32).at[0, :half].set(-1.0).at[0, half:2 * half].set(1.0)
    pos_b = jnp.broadcast_to(positions.astype(F32).reshape(t, 1), (t, LANES))

    x2d, h2d = _even_layer(x2d, mem2d, pos_b, freq_row, sign_row, batch, e_norm_w[0], e_w_in[0],
                           e_fox_b_f[0], e_mla_q_norm_w[0], e_mla_w_uq[0], e_mla_kv_norm_w[0],
                           e_mla_w_ukv[0], e_mem_norm_w[0], e_w_mem_kv[0], e_w_out[0],
                           o_norm_w[0])
    out = _odd_layer(x2d, h2d, mem2d, batch, o_w_in[0], o_gla_w_a2[0], o_gla_b_a[0],
                     o_gla_norm_w[0], o_mem_norm_w[0], o_w_mem_kv[0], o_w_out[0], final_norm_w)
    return out.reshape(batch, seq, d)
```

```python
import functools

import jax
import jax.numpy as jnp
from jax import lax
from jax.experimental import pallas as pl
from jax.experimental.pallas import tpu as pltpu

F32 = jnp.float32
BF16 = jnp.bfloat16

D_MODEL = 2048
CHUNK = 64
EPS = 1e-6
NEG = -1e30

FOX_HEADS = 8
FOX_HEAD_DIM = 128
FOX_WIDTH = FOX_HEADS * FOX_HEAD_DIM

MLA_HEADS = 8
MLA_NOPE = 128
MLA_ROPE = 64
MLA_V = 128
MLA_Q_RANK = 512
MLA_KV_RANK = 256
MLA_WIDTH = MLA_HEADS * MLA_V
MLA_QK_PAD = 256
ROPE_THETA = 10000.0

GLA_HEADS = 4
GLA_KEY_WIDTH = D_MODEL // 2
GLA_VAL_WIDTH = D_MODEL
GLA_DK = GLA_KEY_WIDTH // GLA_HEADS
GLA_DV = GLA_VAL_WIDTH // GLA_HEADS
GLA_GATE_RANK = 16
GLA_TAU = 16.0

MEM_HEADS = 4
MEM_HEAD_DIM = 128
MEM_WIDTH = MEM_HEADS * MEM_HEAD_DIM

LANES = 128
VMEM_LIMIT_BYTES = 56 * 1024 * 1024
PROJ_ROWS = 256
PROJ_COLS = 256
OUT_ROWS = 512
PREP_ROWS = 512
ATTN_Q_ROWS = 256
ATTN_K_COLS = 256
ATTN_HEADS_PER_STEP = 2
LOG2E = 1.4426950408889634
GLA_BLOCK_ROWS = 256


def _params(*sem):
    return pltpu.CompilerParams(dimension_semantics=sem, vmem_limit_bytes=VMEM_LIMIT_BYTES)


def _rms(x, w):
    return x * lax.rsqrt(jnp.mean(x * x, axis=-1, keepdims=True) + EPS) * w


def _log_sigmoid(z):
    return jnp.minimum(z, 0.0) - jnp.log(1.0 + jnp.exp(-jnp.abs(z)))


def _silu(g):
    return g * jax.nn.sigmoid(g)


def _norm_proj_kernel(x_ref, *rest, n_groups, with_norm, with_tail):
    rest = list(rest)
    nw_ref = rest.pop(0) if with_norm else None
    w_refs = [rest.pop(0) for _ in range(n_groups)]
    cs_ref = rest.pop(0)
    wt_ref = rest.pop(0) if with_tail else None
    o_ref = rest.pop(0)
    ot_ref = rest.pop(0) if with_tail else None
    if with_norm:
        (h_ref,) = rest
        h_ref[...] = _rms(x_ref[...], nw_ref[...]).astype(BF16)
    else:
        h_ref = x_ref
    if with_tail:
        ot_ref[...] = jnp.dot(h_ref[...], wt_ref[...], preferred_element_type=F32)
    c0 = 0
    for w_ref in w_refs:
        for c in range(0, w_ref.shape[1], PROJ_COLS):
            acc = jnp.dot(h_ref[...], w_ref[:, c:c + PROJ_COLS], preferred_element_type=F32)
            dst = slice(c0 + c, c0 + c + PROJ_COLS)
            o_ref[:, dst] = (acc * cs_ref[:, dst]).astype(o_ref.dtype)
        c0 += w_ref.shape[1]


def _norm_proj(x, norm_w, ws, col_scale, w_tail, *, rows):
    t, k = x.shape
    n = sum(w.shape[1] for w in ws)
    with_norm = norm_w is not None
    with_tail = w_tail is not None

    def resident(shape):
        return pl.BlockSpec(shape, lambda i: (0, 0), pipeline_mode=pl.Buffered(1))

    in_specs = [pl.BlockSpec((rows, k), lambda i: (i, 0))]
    args = [x]
    if with_norm:
        in_specs.append(resident((1, k)))
        args.append(norm_w.reshape(1, k))
    in_specs += [resident(w.shape) for w in ws] + [resident((1, n))]
    args += list(ws) + [col_scale]
    out_shape = [jax.ShapeDtypeStruct((t, n), BF16)]
    out_specs = [pl.BlockSpec((rows, n), lambda i: (i, 0))]
    if with_tail:
        in_specs.append(resident((k, LANES)))
        args.append(w_tail)
        out_shape.append(jax.ShapeDtypeStruct((t, LANES), F32))
        out_specs.append(pl.BlockSpec((rows, LANES), lambda i: (i, 0)))
    outs = pl.pallas_call(
        functools.partial(_norm_proj_kernel, n_groups=len(ws), with_norm=with_norm,
                          with_tail=with_tail),
        grid=(t // rows,),
        in_specs=in_specs,
        out_specs=out_specs,
        out_shape=out_shape,
        scratch_shapes=[pltpu.VMEM((rows, k), BF16)] if with_norm else [],
        compiler_params=_params("parallel"),
        name="norm_proj",
    )(*args)
    return outs if with_tail else outs[0]


def _fox_gate_kernel(tail_ref, bias_ref, o_ref, *, lane0, heads):
    seq = tail_ref.shape[1]
    logf = _log_sigmoid(tail_ref[0] + bias_ref[...])
    c = logf.T[lane0:lane0 + heads, :]
    lane = lax.broadcasted_iota(jnp.int32, c.shape, 1)
    shift = 1
    while shift < seq:
        c = c + jnp.where(lane >= shift, pltpu.roll(c, shift, axis=1), 0.0)
        shift *= 2
    o_ref[0] = c * -LOG2E


def _fox_gate(tail, bias_row, *, lane0, heads):
    b, s, _ = tail.shape
    return pl.pallas_call(
        functools.partial(_fox_gate_kernel, lane0=lane0, heads=heads),
        grid=(b,),
        in_specs=[pl.BlockSpec((1, s, LANES), lambda i: (i, 0, 0)),
                  pl.BlockSpec((1, LANES), lambda i: (0, 0))],
        out_specs=pl.BlockSpec((1, heads, s), lambda i: (i, 0, 0)),
        out_shape=jax.ShapeDtypeStruct((b, heads, s), F32),
        compiler_params=_params("parallel"),
        name="fox_gate",
    )(tail, bias_row)


def _attn_kernel(q_ref, k_ref, v_ref, g_ref, *rest, mask, q_rows, with_bias, hps):
    if with_bias:
        b_ref, o_ref, s_ref, v1_ref = rest
    else:
        o_ref, s_ref, v1_ref = rest
    seq, sk = q_ref.shape[1], k_ref.shape[1]
    dq, dv = q_ref.shape[2] // hps, v_ref.shape[2] // hps
    for hh in range(hps):
        v1_ref[hh, :, :dv] = v_ref[0, :, hh * dv:(hh + 1) * dv]
        v1_ref[hh, :, dv:] = jnp.ones((sk, dv), BF16)
    k_cols = ATTN_K_COLS
    nt = (((1,), (1,)), ((), ()))

    def lane_slabs(t):
        return [t[:, c:c + LANES] for c in range(0, t.shape[1], LANES)]

    tasks = [(hh, i) for hh in range(hps) for i in range(seq // q_rows)]

    def tiles(i):
        return range(0, sk if mask is None else (i + 1) * q_rows, k_cols)

    def pass1(n):
        hh, i = tasks[n]
        lo = i * q_rows
        q = q_ref[0, lo:lo + q_rows, hh * dq:(hh + 1) * dq]
        mx = jnp.full((q_rows, LANES), NEG, F32)
        for a in tiles(i):
            s = lax.dot_general(q, k_ref[0, a:a + k_cols, hh * dq:(hh + 1) * dq], nt,
                                preferred_element_type=F32)
            if with_bias:
                s = s + b_ref[hh, :, a:a + k_cols]
            if mask is not None and a + k_cols > lo + 1:
                row = lo + lax.broadcasted_iota(jnp.int32, s.shape, 0)
                col = a + lax.broadcasted_iota(jnp.int32, s.shape, 1)
                if mask == "chunk":
                    shift = CHUNK.bit_length() - 1
                    row, col = row >> shift, col >> shift
                s = jnp.where(row >= col, s, NEG)
            s_ref[n % 2, :, a:a + k_cols] = s
            mx = functools.reduce(jnp.maximum, lane_slabs(s), mx)
        return jnp.max(mx, axis=-1, keepdims=True)

    def pass2(n, m):
        hh, i = tasks[n]
        lo = i * q_rows
        o = jnp.zeros((q_rows, 2 * dv), F32)
        for a in tiles(i):
            p = jnp.exp2(s_ref[n % 2, :, a:a + k_cols] - m)
            o = o + jnp.dot(p.astype(BF16), v1_ref[hh, a:a + k_cols, :],
                            preferred_element_type=F32)
        g = g_ref[0, lo:lo + q_rows, hh * dv:(hh + 1) * dv].astype(F32)
        o_ref[0, lo:lo + q_rows, hh * dv:(hh + 1) * dv] = (
            o[:, :dv] / o[:, dv:] * _silu(g)).astype(o_ref.dtype)

    m = pass1(0)
    for n in range(len(tasks)):
        m_next = pass1(n + 1) if n + 1 < len(tasks) else None
        pass2(n, m)
        m = m_next


def _attention(q, k, v, gate, bias, *, heads, dq, dv, q_blk, k_blk, v_blk, g_blk,
               mask, q_rows, name):
    b, s, _ = q.shape
    sk = k.shape[1]
    hps = ATTN_HEADS_PER_STEP
    assert heads % hps == q_blk % hps == k_blk % hps == v_blk % hps == g_blk % hps == 0
    groups = heads // hps
    in_specs = [
        pl.BlockSpec((1, s, hps * dq), lambda i, h: (i, 0, q_blk // hps + h)),
        pl.BlockSpec((1, sk, hps * dq), lambda i, h: (i, 0, k_blk // hps + h)),
        pl.BlockSpec((1, sk, hps * dv), lambda i, h: (i, 0, v_blk // hps + h)),
        pl.BlockSpec((1, s, hps * dv), lambda i, h: (i, 0, g_blk // hps + h)),
    ]
    args = [q, k, v, gate]
    if bias is not None:
        in_specs.append(pl.BlockSpec((hps, 1, sk), lambda i, h: (i * groups + h, 0, 0)))
        args.append(bias)
    return pl.pallas_call(
        functools.partial(_attn_kernel, mask=mask, q_rows=q_rows, with_bias=bias is not None,
                          hps=hps),
        grid=(b, groups),
        in_specs=in_specs,
        out_specs=pl.BlockSpec((1, s, hps * dv), lambda i, h: (i, 0, h)),
        out_shape=jax.ShapeDtypeStruct((b, s, heads * dv), BF16),
        scratch_shapes=[pltpu.VMEM((2, q_rows, sk), F32),
                        pltpu.VMEM((hps, sk, 2 * dv), BF16)],
        compiler_params=_params("parallel", "arbitrary"),
        name=name,
    )(*args)


def _rope(v, cos, sin_signed, lane):
    swapped = jnp.where(lane < MLA_ROPE // 2,
                        pltpu.roll(v, LANES - MLA_ROPE // 2, axis=1),
                        pltpu.roll(v, MLA_ROPE // 2, axis=1))
    return v * cos + swapped * sin_signed


def _mla_prep_kernel(cq_ref, ckv_ref, tail_ref, pos_ref, freq_ref, sign_ref, qnw_ref, kvnw_ref,
                     wuq_ref, wukv_ref, q_ref, k_ref, v_ref):
    rows = cq_ref.shape[0]
    ang = pos_ref[...] * freq_ref[...]
    cos = jnp.cos(ang)
    sin_signed = jnp.sin(ang) * sign_ref[...]
    lane = lax.broadcasted_iota(jnp.int32, (rows, LANES), 1)
    scale = (MLA_NOPE + MLA_ROPE) ** -0.5 * LOG2E

    cqn = _rms(cq_ref[...].astype(F32), qnw_ref[...]).astype(BF16)
    q = jnp.dot(cqn, wuq_ref[...], preferred_element_type=F32) * scale
    ckvn = _rms(ckv_ref[...].astype(F32), kvnw_ref[...]).astype(BF16)
    kv = jnp.dot(ckvn, wukv_ref[...], preferred_element_type=F32)
    k_rope = jnp.where(lane < MLA_ROPE, _rope(tail_ref[...], cos, sin_signed, lane), 0.0)
    k_rope = k_rope.astype(BF16)
    for h in range(MLA_HEADS):
        c0 = h * MLA_QK_PAD
        q_ref[:, c0:c0 + MLA_NOPE] = q[:, c0:c0 + MLA_NOPE].astype(BF16)
        q_ref[:, c0 + MLA_NOPE:c0 + MLA_QK_PAD] = _rope(
            q[:, c0 + MLA_NOPE:c0 + MLA_QK_PAD], cos, sin_signed, lane).astype(BF16)
        k_ref[:, c0:c0 + MLA_NOPE] = kv[:, h * MLA_NOPE:(h + 1) * MLA_NOPE].astype(BF16)
        k_ref[:, c0 + MLA_NOPE:c0 + MLA_QK_PAD] = k_rope
    v_ref[...] = kv[:, MLA_HEADS * MLA_NOPE:].astype(BF16)


def _mla_prep(main, tail, pos_b, freq_row, sign_row, q_norm_w, kv_norm_w, w_uq, w_ukv,
              *, cq_col, ckv_col):
    t = main.shape[0]
    rows = PREP_ROWS
    const = lambda shape: pl.BlockSpec(shape, lambda i: (0, 0))
    return pl.pallas_call(
        _mla_prep_kernel,
        grid=(t // rows,),
        in_specs=[
            pl.BlockSpec((rows, MLA_Q_RANK), lambda i: (i, cq_col // MLA_Q_RANK)),
            pl.BlockSpec((rows, MLA_KV_RANK), lambda i: (i, ckv_col // MLA_KV_RANK)),
            pl.BlockSpec((rows, LANES), lambda i: (i, 0)),
            pl.BlockSpec((rows, LANES), lambda i: (i, 0)),
            const((1, LANES)), const((1, LANES)),
            const((1, MLA_Q_RANK)), const((1, MLA_KV_RANK)),
            const(w_uq.shape), const(w_ukv.shape),
        ],
        out_specs=[
            pl.BlockSpec((rows, MLA_HEADS * MLA_QK_PAD), lambda i: (i, 0)),
            pl.BlockSpec((rows, MLA_HEADS * MLA_QK_PAD), lambda i: (i, 0)),
            pl.BlockSpec((rows, MLA_WIDTH), lambda i: (i, 0)),
        ],
        out_shape=[
            jax.ShapeDtypeStruct((t, MLA_HEADS * MLA_QK_PAD), BF16),
            jax.ShapeDtypeStruct((t, MLA_HEADS * MLA_QK_PAD), BF16),
            jax.ShapeDtypeStruct((t, MLA_WIDTH), BF16),
        ],
        compiler_params=_params("parallel"),
        name="mla_prep",
    )(main, main, tail, pos_b, freq_row, sign_row, q_norm_w.reshape(1, -1),
      kv_norm_w.reshape(1, -1), w_uq, w_ukv)


def _gla_kernel(q_ref, k_ref, v_ref, gg_ref, ga_ref, wa_ref, ba_ref, nw_ref, o_ref,
                hl_ref, qd_ref, ki_ref, ke_ref, qx_ref, qb_ref, kb_ref, dec_ref, att_ref,
                oi_ref, kv_ref, sall_ref):
    seq, dk = q_ref.shape[1], q_ref.shape[2]
    rows = GLA_BLOCK_ROWS
    n_c = rows // CHUNK
    n_x = sum(rows - d * CHUNK for d in range(1, n_c))
    shift = CHUNK.bit_length() - 1
    row = lax.broadcasted_iota(jnp.int32, (rows, rows), 0)
    col = lax.broadcasted_iota(jnp.int32, (rows, rows), 1)
    chunk_gap = (row >> shift) - (col >> shift)
    causal = jnp.logical_and(chunk_gap == 0, row >= col)
    tri = causal.astype(BF16)
    ones = (chunk_gap == 0).astype(BF16)
    nt = (((1,), (1,)), ((), ()))
    tn = (((0,), (0,)), ((), ()))
    blocks = list(range(seq // rows))

    def rows_of(b, lo=0, hi=rows):
        return slice(b * rows + lo, b * rows + hi)

    for b in blocks:
        r = rows_of(b)
        z = jnp.dot(ga_ref[0, r, :].astype(BF16), wa_ref[...], preferred_element_type=F32)
        log_a = _log_sigmoid(z + ba_ref[...]) / GLA_TAU
        hi = log_a.astype(BF16)
        hl_ref[r, :dk] = hi
        hl_ref[r, dk:] = (log_a - hi.astype(F32)).astype(BF16)

    for b in blocks:
        r = rows_of(b)
        hl = hl_ref[r, :]
        cum2 = jnp.dot(tri, hl, preferred_element_type=F32)
        tot2 = jnp.dot(ones, hl, preferred_element_type=F32)
        cum = cum2[:, :dk] + cum2[:, dk:]
        tot = tot2[:, :dk] + tot2[:, dk:]
        q_dec = q_ref[0, r, :].astype(F32) * jnp.exp(cum)
        k = k_ref[0, r, :].astype(F32)
        k_end = k * jnp.exp(tot - cum)
        qd_ref[r, :] = q_dec.astype(BF16)
        ki_ref[r, :] = (k * jnp.exp(-cum)).astype(BF16)
        ke_ref[r, :] = k_end.astype(BF16)
        t = [tot[c * CHUNK:c * CHUNK + 1, :] for c in range(n_c)]
        x0 = b * n_x
        for d in range(1, n_c):
            for c in range(d, n_c):
                rc = slice(c * CHUNK, (c + 1) * CHUNK)
                gap = sum(t[c - d + 1:c], jnp.zeros_like(t[0]))
                qx_ref[x0 + (c - d) * CHUNK:x0 + (c - d + 1) * CHUNK, :] = (
                    q_dec[rc] * jnp.exp(gap)).astype(BF16)
            x0 += rows - d * CHUNK
        for c in range(n_c):
            rc = slice(c * CHUNK, (c + 1) * CHUNK)
            before = sum(t[:c], jnp.zeros_like(t[0]))
            after = sum(t[c + 1:], jnp.zeros_like(t[0]))
            qb_ref[rows_of(b, c * CHUNK, (c + 1) * CHUNK), :] = (
                q_dec[rc] * jnp.exp(before)).astype(BF16)
            kb_ref[rows_of(b, c * CHUNK, (c + 1) * CHUNK), :] = (
                k_end[rc] * jnp.exp(after)).astype(BF16)
        dec_ref[b] = jnp.broadcast_to(jnp.exp(sum(t[1:], t[0])), dec_ref.shape[1:])

    for b in blocks:
        r = rows_of(b)
        same = lax.dot_general(qd_ref[r, :], ki_ref[r, :], nt, preferred_element_type=F32)
        cross = lax.dot_general(qx_ref[b * n_x:(b + 1) * n_x, :], ke_ref[r, :], nt,
                                preferred_element_type=F32)
        att = jnp.where(causal, same, 0.0)
        x0 = 0
        for d in range(1, n_c):
            part = cross[x0:x0 + rows - d * CHUNK]
            part = jnp.concatenate([jnp.zeros((d * CHUNK, rows), F32), part], axis=0)
            att = jnp.where(chunk_gap == d, part, att)
            x0 += rows - d * CHUNK
        att_ref[r, :] = att.astype(BF16)

    for b in blocks:
        r = rows_of(b)
        v = v_ref[0, r, :]
        oi_ref[r, :] = jnp.dot(att_ref[r, :], v, preferred_element_type=F32)
        kv_ref[b] = lax.dot_general(v, kb_ref[r, :], tn, preferred_element_type=F32)

    state = jnp.zeros(kv_ref.shape[1:], F32)
    for b in blocks:
        sall_ref[b] = state.astype(BF16)
        if b + 1 < len(blocks):
            state = state * dec_ref[b, 0:1, :] + kv_ref[b]

    def inter(b):
        return lax.dot_general(qb_ref[rows_of(b), :], sall_ref[b], nt,
                               preferred_element_type=F32)

    o_inter = inter(0)
    for b in blocks:
        o_next = inter(b + 1) if b + 1 < len(blocks) else None
        for c in range(n_c):
            rc = rows_of(b, c * CHUNK, (c + 1) * CHUNK)
            o = oi_ref[rc, :] + o_inter[c * CHUNK:(c + 1) * CHUNK]
            g = gg_ref[0, rc, :].astype(F32)
            o_ref[0, rc, :] = (_rms(o, nw_ref[...]) * _silu(g)).astype(o_ref.dtype)
        o_inter = o_next


def _gla(main, tail, w_a2, b_a, norm_w):
    b, s, _ = main.shape
    n_blocks = s // GLA_BLOCK_ROWS
    n_cross = sum(GLA_BLOCK_ROWS - d * CHUNK for d in range(1, GLA_BLOCK_ROWS // CHUNK))
    return pl.pallas_call(
        _gla_kernel,
        grid=(b, GLA_HEADS),
        in_specs=[
            pl.BlockSpec((1, s, GLA_DK), lambda i, h: (i, 0, h)),
            pl.BlockSpec((1, s, GLA_DK), lambda i, h: (i, 0, GLA_HEADS + h)),
            pl.BlockSpec((1, s, GLA_DV), lambda i, h: (i, 0, GLA_HEADS + h)),
            pl.BlockSpec((1, s, GLA_DV), lambda i, h: (i, 0, 2 * GLA_HEADS + h)),
            pl.BlockSpec((1, s, LANES), lambda i, h: (i, 0, 0)),
            pl.BlockSpec((LANES, GLA_DK), lambda i, h: (0, h)),
            pl.BlockSpec((1, GLA_DK), lambda i, h: (0, h)),
            pl.BlockSpec((1, GLA_DV), lambda i, h: (0, 0)),
        ],
        out_specs=pl.BlockSpec((1, s, GLA_DV), lambda i, h: (i, 0, h)),
        out_shape=jax.ShapeDtypeStruct((b, s, GLA_VAL_WIDTH), BF16),
        scratch_shapes=[pltpu.VMEM((s, 2 * GLA_DK), BF16),
                        pltpu.VMEM((s, GLA_DK), BF16),
                        pltpu.VMEM((s, GLA_DK), BF16),
                        pltpu.VMEM((s, GLA_DK), BF16),
                        pltpu.VMEM((n_blocks * n_cross, GLA_DK), BF16),
                        pltpu.VMEM((s, GLA_DK), BF16),
                        pltpu.VMEM((s, GLA_DK), BF16),
                        pltpu.VMEM((n_blocks, 8, GLA_DK), F32),
                        pltpu.VMEM((s, GLA_BLOCK_ROWS), BF16),
                        pltpu.VMEM((s, GLA_DV), F32),
                        pltpu.VMEM((n_blocks, GLA_DV, GLA_DK), F32),
                        pltpu.VMEM((n_blocks, GLA_DV, GLA_DK), BF16)],
        compiler_params=_params("parallel", "arbitrary"),
        name="gla",
    )(main, main, main, main, tail, w_a2, b_a.reshape(1, -1), norm_w.reshape(1, -1))


def _out_proj_kernel(*refs, n_parts, keep_residual):
    parts = refs[:n_parts]
    x_ref, w_ref, nw_ref = refs[n_parts:n_parts + 3]
    outs = refs[n_parts + 3:]
    y = x_ref[...]
    r0 = 0
    for p in parts:
        width = p.shape[1]
        y = y + jnp.dot(p[...], w_ref[r0:r0 + width, :], preferred_element_type=F32)
        r0 += width
    if keep_residual:
        y_ref, h_ref = outs
        y_ref[...] = y
    else:
        (h_ref,) = outs
    h_ref[...] = _rms(y, nw_ref[...]).astype(h_ref.dtype)


def _out_proj(parts, x, w, norm_w, *, keep_residual):
    t, d = x.shape
    rows = OUT_ROWS
    in_specs = [pl.BlockSpec((rows, p.shape[1]), lambda i: (i, 0)) for p in parts]
    in_specs += [pl.BlockSpec((rows, d), lambda i: (i, 0)),
                 pl.BlockSpec(w.shape, lambda i: (0, 0), pipeline_mode=pl.Buffered(1)),
                 pl.BlockSpec((1, d), lambda i: (0, 0))]
    args = list(parts) + [x, w, norm_w.reshape(1, d)]
    row_spec = pl.BlockSpec((rows, d), lambda i: (i, 0))
    if keep_residual:
        out_specs = [row_spec, row_spec]
        out_shape = [jax.ShapeDtypeStruct((t, d), F32), jax.ShapeDtypeStruct((t, d), BF16)]
    else:
        out_specs = row_spec
        out_shape = jax.ShapeDtypeStruct((t, d), F32)
    return pl.pallas_call(
        functools.partial(_out_proj_kernel, n_parts=len(parts), keep_residual=keep_residual),
        grid=(t // rows,),
        in_specs=in_specs,
        out_specs=out_specs,
        out_shape=out_shape,
        compiler_params=_params("parallel"),
        name="out_proj",
    )(*args)


def _memory_branch(main, mem2d, mem_norm_w, w_mem_kv, *, batch, q_col, g_col):
    n = w_mem_kv.shape[1]
    mem_kv = _norm_proj(mem2d, mem_norm_w, [w_mem_kv.astype(BF16)], jnp.ones((1, n), F32), None,
                        rows=PROJ_ROWS)
    mem_kv = mem_kv.reshape(batch, -1, n)
    return _attention(main, mem_kv, mem_kv, main, None, heads=MEM_HEADS, dq=MEM_HEAD_DIM,
                      dv=MEM_HEAD_DIM, q_blk=q_col // MEM_HEAD_DIM, k_blk=0, v_blk=MEM_HEADS,
                      g_blk=g_col // MEM_HEAD_DIM, mask=None, q_rows=ATTN_Q_ROWS,
                      name="mem_attn")


def _col_scale(widths_and_scales):
    return jnp.concatenate([jnp.full((1, w), s, F32) for w, s in widths_and_scales], axis=1)


def _split_cols(w, sizes):
    out, acc = [], 0
    for s in sizes:
        out.append(w[:, acc:acc + s].astype(BF16))
        acc += s
    return out


def _even_layer(x2d, mem2d, pos_b, freq_row, sign_row, batch, norm_w, w_in, fox_b_f,
                mla_q_norm_w, mla_w_uq, mla_kv_norm_w, mla_w_ukv, mem_norm_w, w_mem_kv, w_out,
                next_norm_w):
    t = x2d.shape[0]
    seq = t // batch
    (w_a, w_ff, w_b, w_kr, w_c) = _split_cols(
        w_in, (3 * FOX_WIDTH, FOX_HEADS, FOX_WIDTH + MLA_Q_RANK + MLA_KV_RANK,
               MLA_ROPE, MLA_WIDTH + 2 * MEM_WIDTH))
    fq_col, fk_col, fv_col, fg_col = 0, FOX_WIDTH, 2 * FOX_WIDTH, 3 * FOX_WIDTH
    cq_col = 4 * FOX_WIDTH
    ckv_col = cq_col + MLA_Q_RANK
    mg_col = ckv_col + MLA_KV_RANK
    mq_col = mg_col + MLA_WIDTH
    memg_col = mq_col + MEM_WIDTH
    n_main = memg_col + MEM_WIDTH
    ff_lane = MLA_ROPE
    w_tail = jnp.concatenate(
        [w_kr, w_ff, jnp.zeros((D_MODEL, LANES - MLA_ROPE - FOX_HEADS), BF16)], axis=1)
    col_scale = _col_scale([(FOX_WIDTH, FOX_HEAD_DIM ** -0.5 * LOG2E), (mq_col - FOX_WIDTH, 1.0),
                            (MEM_WIDTH, MEM_HEAD_DIM ** -0.5 * LOG2E), (MEM_WIDTH, 1.0)])
    main, tail = _norm_proj(x2d, norm_w, [w_a, w_b, w_c], col_scale, w_tail, rows=PROJ_ROWS)
    main3 = main.reshape(batch, seq, n_main)

    bias_row = jnp.zeros((1, LANES), F32).at[0, ff_lane:ff_lane + FOX_HEADS].set(fox_b_f)
    neg_c = _fox_gate(tail.reshape(batch, seq, LANES), bias_row, lane0=ff_lane, heads=FOX_HEADS)
    o_a = _attention(main3, main3, main3, main3, neg_c.reshape(batch * FOX_HEADS, 1, seq),
                     heads=FOX_HEADS, dq=FOX_HEAD_DIM, dv=FOX_HEAD_DIM,
                     q_blk=fq_col // FOX_HEAD_DIM, k_blk=fk_col // FOX_HEAD_DIM,
                     v_blk=fv_col // FOX_HEAD_DIM, g_blk=fg_col // FOX_HEAD_DIM,
                     mask="causal", q_rows=ATTN_Q_ROWS, name="fox_attn")

    pad = jnp.zeros((MLA_Q_RANK, MLA_HEADS, MLA_QK_PAD - MLA_NOPE - MLA_ROPE), F32)
    w_uq = jnp.concatenate([mla_w_uq.reshape(MLA_Q_RANK, MLA_HEADS, MLA_NOPE + MLA_ROPE), pad],
                           axis=2).reshape(MLA_Q_RANK, MLA_HEADS * MLA_QK_PAD).astype(BF16)
    w_ukv = mla_w_ukv.reshape(MLA_KV_RANK, MLA_HEADS, MLA_NOPE + MLA_V)
    w_ukv = jnp.concatenate([w_ukv[:, :, :MLA_NOPE].reshape(MLA_KV_RANK, -1),
                             w_ukv[:, :, MLA_NOPE:].reshape(MLA_KV_RANK, -1)], axis=1).astype(BF16)
    q_cat, k_cat, v_b = _mla_prep(main, tail, pos_b, freq_row, sign_row, mla_q_norm_w,
                                  mla_kv_norm_w, w_uq, w_ukv, cq_col=cq_col, ckv_col=ckv_col)
    o_b = _attention(q_cat.reshape(batch, seq, -1), k_cat.reshape(batch, seq, -1),
                     v_b.reshape(batch, seq, -1), main3, None, heads=MLA_HEADS, dq=MLA_QK_PAD,
                     dv=MLA_V, q_blk=0, k_blk=0, v_blk=0, g_blk=mg_col // MLA_V,
                     mask="chunk", q_rows=ATTN_Q_ROWS, name="mla_attn")

    o_m = _memory_branch(main3, mem2d, mem_norm_w, w_mem_kv, batch=batch, q_col=mq_col,
                         g_col=memg_col)
    parts = [o_a.reshape(t, -1), o_b.reshape(t, -1), o_m.reshape(t, -1)]
    return _out_proj(parts, x2d, w_out.astype(BF16), next_norm_w, keep_residual=True)


def _odd_layer(x2d, h2d, mem2d, batch, w_in, gla_w_a2, gla_b_a, gla_norm_w, mem_norm_w,
               w_mem_kv, w_out, final_w):
    t = x2d.shape[0]
    seq = t // batch
    (w_a, w_ga, w_b) = _split_cols(
        w_in, (2 * GLA_KEY_WIDTH + GLA_VAL_WIDTH, GLA_GATE_RANK,
               GLA_VAL_WIDTH + 2 * MEM_WIDTH))
    mq_col = 2 * GLA_KEY_WIDTH + 2 * GLA_VAL_WIDTH
    memg_col = mq_col + MEM_WIDTH
    n_main = memg_col + MEM_WIDTH
    w_tail = jnp.concatenate([w_ga, jnp.zeros((D_MODEL, LANES - GLA_GATE_RANK), BF16)], axis=1)
    col_scale = _col_scale([(GLA_KEY_WIDTH, GLA_DK ** -0.5), (mq_col - GLA_KEY_WIDTH, 1.0),
                            (MEM_WIDTH, MEM_HEAD_DIM ** -0.5 * LOG2E), (MEM_WIDTH, 1.0)])
    main, tail = _norm_proj(h2d, None, [w_a, w_b], col_scale, w_tail, rows=PROJ_ROWS)
    main3 = main.reshape(batch, seq, n_main)

    w_a2 = jnp.concatenate([gla_w_a2, jnp.zeros((LANES - GLA_GATE_RANK, GLA_KEY_WIDTH), F32)],
                           axis=0).astype(BF16)
    o_c = _gla(main3, tail.reshape(batch, seq, LANES), w_a2, gla_b_a, gla_norm_w)
    o_m = _memory_branch(main3, mem2d, mem_norm_w, w_mem_kv, batch=batch, q_col=mq_col,
                         g_col=memg_col)
    parts = [o_c.reshape(t, -1), o_m.reshape(t, -1)]
    return _out_proj(parts, x2d, w_out.astype(BF16), final_w, keep_residual=False)


def kernel(x, mem, positions, e_norm_w, e_w_in, e_fox_b_f, e_mla_q_norm_w, e_mla_w_uq, e_mla_kv_norm_w, e_mla_w_ukv, e_mem_norm_w, e_w_mem_kv, e_w_out, o_norm_w, o_w_in, o_gla_w_a2, o_gla_b_a, o_gla_norm_w, o_mem_norm_w, o_w_mem_kv, o_w_out, final_norm_w):
    batch, seq, d = x.shape
    t = batch * seq
    x2d = x.reshape(t, d)
    mem2d = mem.reshape(-1, d)
    inv_freq = 1.0 / (ROPE_THETA ** (jnp.arange(0, MLA_ROPE, 2, dtype=F32) / MLA_ROPE))
    half = MLA_ROPE // 2
    freq_row = jnp.zeros((1, LANES), F32).at[0, :half].set(inv_freq).at[0, half:2 * half].set(inv_freq)
    sign_row = jnp.zeros((1, LANES), F32).at[0, :half].set(-1.0).at[0, half:2 * half].set(1.0)
    pos_b = jnp.broadcast_to(positions.astype(F32).reshape(t, 1), (t, LANES))

    x2d, h2d = _even_layer(x2d, mem2d, pos_b, freq_row, sign_row, batch, e_norm_w[0], e_w_in[0],
                           e_fox_b_f[0], e_mla_q_norm_w[0], e_mla_w_uq[0], e_mla_kv_norm_w[0],
                           e_mla_w_ukv[0], e_mem_norm_w[0], e_w_mem_kv[0], e_w_out[0],
                           o_norm_w[0])
    out = _odd_layer(x2d, h2d, mem2d, batch, o_w_in[0], o_gla_w_a2[0], o_gla_b_a[0],
                     o_gla_norm_w[0], o_mem_norm_w[0], o_w_mem_kv[0], o_w_out[0], final_norm_w)
    return out.reshape(batch, seq, d)
```

```python
import functools

import jax
import jax.numpy as jnp
from jax import lax
from jax.experimental import pallas as pl
from jax.experimental.pallas import tpu as pltpu

F32 = jnp.float32
BF16 = jnp.bfloat16

D_MODEL = 2048
CHUNK = 64
EPS = 1e-6
NEG = -1e30

FOX_HEADS = 8
FOX_HEAD_DIM = 128
FOX_WIDTH = FOX_HEADS * FOX_HEAD_DIM

MLA_HEADS = 8
MLA_NOPE = 128
MLA_ROPE = 64
MLA_V = 128
MLA_Q_RANK = 512
MLA_KV_RANK = 256
MLA_WIDTH = MLA_HEADS * MLA_V
MLA_QK_PAD = 256
ROPE_THETA = 10000.0

GLA_HEADS = 4
GLA_KEY_WIDTH = D_MODEL // 2
GLA_VAL_WIDTH = D_MODEL
GLA_DK = GLA_KEY_WIDTH // GLA_HEADS
GLA_DV = GLA_VAL_WIDTH // GLA_HEADS
GLA_GATE_RANK = 16
GLA_TAU = 16.0

MEM_HEADS = 4
MEM_HEAD_DIM = 128
MEM_WIDTH = MEM_HEADS * MEM_HEAD_DIM

LANES = 128
VMEM_LIMIT_BYTES = 56 * 1024 * 1024
PROJ_ROWS = 256
PROJ_COLS = 256
OUT_ROWS = 512
OUT_SUB_ROWS = 256
PREP_ROWS = 512
ATTN_Q_ROWS = 256
ATTN_K_COLS = 256
ATTN_HEADS_PER_STEP = 2
LOG2E = 1.4426950408889634
GLA_BLOCK_ROWS = 256


def _params(*sem):
    return pltpu.CompilerParams(dimension_semantics=sem, vmem_limit_bytes=VMEM_LIMIT_BYTES)


def _rms(x, w):
    return x * lax.rsqrt(jnp.mean(x * x, axis=-1, keepdims=True) + EPS) * w


def _log_sigmoid(z):
    return jnp.minimum(z, 0.0) - jnp.log(1.0 + jnp.exp(-jnp.abs(z)))


def _silu(g):
    return g * jax.nn.sigmoid(g)


def _norm_proj_kernel(x_ref, *rest, n_groups, with_norm, with_tail):
    rest = list(rest)
    nw_ref = rest.pop(0) if with_norm else None
    w_refs = [rest.pop(0) for _ in range(n_groups)]
    cs_ref = rest.pop(0)
    wt_ref = rest.pop(0) if with_tail else None
    o_ref = rest.pop(0)
    ot_ref = rest.pop(0) if with_tail else None
    if with_norm:
        (h_ref,) = rest
        h_ref[...] = _rms(x_ref[...], nw_ref[...]).astype(BF16)
    else:
        h_ref = x_ref
    if with_tail:
        ot_ref[...] = jnp.dot(h_ref[...], wt_ref[...], preferred_element_type=F32)
    c0 = 0
    for w_ref in w_refs:
        for c in range(0, w_ref.shape[1], PROJ_COLS):
            acc = jnp.dot(h_ref[...], w_ref[:, c:c + PROJ_COLS], preferred_element_type=F32)
            dst = slice(c0 + c, c0 + c + PROJ_COLS)
            o_ref[:, dst] = (acc * cs_ref[:, dst]).astype(o_ref.dtype)
        c0 += w_ref.shape[1]


def _norm_proj(x, norm_w, ws, col_scale, w_tail, *, rows):
    t, k = x.shape
    n = sum(w.shape[1] for w in ws)
    with_norm = norm_w is not None
    with_tail = w_tail is not None

    def resident(shape):
        return pl.BlockSpec(shape, lambda i: (0, 0), pipeline_mode=pl.Buffered(1))

    in_specs = [pl.BlockSpec((rows, k), lambda i: (i, 0))]
    args = [x]
    if with_norm:
        in_specs.append(resident((1, k)))
        args.append(norm_w.reshape(1, k))
    in_specs += [resident(w.shape) for w in ws] + [resident((1, n))]
    args += list(ws) + [col_scale]
    out_shape = [jax.ShapeDtypeStruct((t, n), BF16)]
    out_specs = [pl.BlockSpec((rows, n), lambda i: (i, 0))]
    if with_tail:
        in_specs.append(resident((k, LANES)))
        args.append(w_tail)
        out_shape.append(jax.ShapeDtypeStruct((t, LANES), F32))
        out_specs.append(pl.BlockSpec((rows, LANES), lambda i: (i, 0)))
    outs = pl.pallas_call(
        functools.partial(_norm_proj_kernel, n_groups=len(ws), with_norm=with_norm,
                          with_tail=with_tail),
        grid=(t // rows,),
        in_specs=in_specs,
        out_specs=out_specs,
        out_shape=out_shape,
        scratch_shapes=[pltpu.VMEM((rows, k), BF16)] if with_norm else [],
        compiler_params=_params("parallel"),
        name="norm_proj",
    )(*args)
    return outs if with_tail else outs[0]


def _fox_gate_kernel(tail_ref, bias_ref, o_ref, *, lane0, heads):
    seq = tail_ref.shape[1]
    logf = _log_sigmoid(tail_ref[0] + bias_ref[...])
    c = logf.T[lane0:lane0 + heads, :]
    lane = lax.broadcasted_iota(jnp.int32, c.shape, 1)
    shift = 1
    while shift < seq:
        c = c + jnp.where(lane >= shift, pltpu.roll(c, shift, axis=1), 0.0)
        shift *= 2
    o_ref[0] = c * -LOG2E


def _fox_gate(tail, bias_row, *, lane0, heads):
    b, s, _ = tail.shape
    return pl.pallas_call(
        functools.partial(_fox_gate_kernel, lane0=lane0, heads=heads),
        grid=(b,),
        in_specs=[pl.BlockSpec((1, s, LANES), lambda i: (i, 0, 0)),
                  pl.BlockSpec((1, LANES), lambda i: (0, 0))],
        out_specs=pl.BlockSpec((1, heads, s), lambda i: (i, 0, 0)),
        out_shape=jax.ShapeDtypeStruct((b, heads, s), F32),
        compiler_params=_params("parallel"),
        name="fox_gate",
    )(tail, bias_row)


def _attn_kernel(q_ref, k_ref, v_ref, g_ref, *rest, mask, q_rows, with_bias, hps):
    if with_bias:
        b_ref, o_ref, s_ref, v1_ref = rest
    else:
        o_ref, s_ref, v1_ref = rest
    seq, sk = q_ref.shape[1], k_ref.shape[1]
    dq, dv = q_ref.shape[2] // hps, v_ref.shape[2] // hps
    for hh in range(hps):
        v1_ref[hh, :, :dv] = v_ref[0, :, hh * dv:(hh + 1) * dv]
        v1_ref[hh, :, dv:] = jnp.ones((sk, dv), BF16)
    k_cols = ATTN_K_COLS
    nt = (((1,), (1,)), ((), ()))

    def lane_slabs(t):
        return [t[:, c:c + LANES] for c in range(0, t.shape[1], LANES)]

    tasks = [(hh, i) for hh in range(hps) for i in range(seq // q_rows)]

    def tiles(i):
        return range(0, sk if mask is None else (i + 1) * q_rows, k_cols)

    def pass1(n):
        hh, i = tasks[n]
        lo = i * q_rows
        q = q_ref[0, lo:lo + q_rows, hh * dq:(hh + 1) * dq]
        mx = jnp.full((q_rows, LANES), NEG, F32)
        for a in tiles(i):
            s = lax.dot_general(q, k_ref[0, a:a + k_cols, hh * dq:(hh + 1) * dq], nt,
                                preferred_element_type=F32)
            if with_bias:
                s = s + b_ref[hh, :, a:a + k_cols]
            if mask is not None and a + k_cols > lo + 1:
                row = lo + lax.broadcasted_iota(jnp.int32, s.shape, 0)
                col = a + lax.broadcasted_iota(jnp.int32, s.shape, 1)
                if mask == "chunk":
                    shift = CHUNK.bit_length() - 1
                    row, col = row >> shift, col >> shift
                s = jnp.where(row >= col, s, NEG)
            s_ref[n % 2, :, a:a + k_cols] = s
            mx = functools.reduce(jnp.maximum, lane_slabs(s), mx)
        return jnp.max(mx, axis=-1, keepdims=True)

    def pass2(n, m):
        hh, i = tasks[n]
        lo = i * q_rows
        o = jnp.zeros((q_rows, 2 * dv), F32)
        for a in tiles(i):
            p = jnp.exp2(s_ref[n % 2, :, a:a + k_cols] - m)
            o = o + jnp.dot(p.astype(BF16), v1_ref[hh, a:a + k_cols, :],
                            preferred_element_type=F32)
        g = g_ref[0, lo:lo + q_rows, hh * dv:(hh + 1) * dv].astype(F32)
        o_ref[0, lo:lo + q_rows, hh * dv:(hh + 1) * dv] = (
            o[:, :dv] / o[:, dv:] * _silu(g)).astype(o_ref.dtype)

    m = pass1(0)
    for n in range(len(tasks)):
        m_next = pass1(n + 1) if n + 1 < len(tasks) else None
        pass2(n, m)
        m = m_next


def _attention(q, k, v, gate, bias, *, heads, dq, dv, q_blk, k_blk, v_blk, g_blk,
               mask, q_rows, name):
    b, s, _ = q.shape
    sk = k.shape[1]
    hps = ATTN_HEADS_PER_STEP
    assert heads % hps == q_blk % hps == k_blk % hps == v_blk % hps == g_blk % hps == 0
    groups = heads // hps
    in_specs = [
        pl.BlockSpec((1, s, hps * dq), lambda i, h: (i, 0, q_blk // hps + h)),
        pl.BlockSpec((1, sk, hps * dq), lambda i, h: (i, 0, k_blk // hps + h)),
        pl.BlockSpec((1, sk, hps * dv), lambda i, h: (i, 0, v_blk // hps + h)),
        pl.BlockSpec((1, s, hps * dv), lambda i, h: (i, 0, g_blk // hps + h)),
    ]
    args = [q, k, v, gate]
    if bias is not None:
        in_specs.append(pl.BlockSpec((hps, 1, sk), lambda i, h: (i * groups + h, 0, 0)))
        args.append(bias)
    return pl.pallas_call(
        functools.partial(_attn_kernel, mask=mask, q_rows=q_rows, with_bias=bias is not None,
                          hps=hps),
        grid=(b, groups),
        in_specs=in_specs,
        out_specs=pl.BlockSpec((1, s, hps * dv), lambda i, h: (i, 0, h)),
        out_shape=jax.ShapeDtypeStruct((b, s, heads * dv), BF16),
        scratch_shapes=[pltpu.VMEM((2, q_rows, sk), F32),
                        pltpu.VMEM((hps, sk, 2 * dv), BF16)],
        compiler_params=_params("parallel", "arbitrary"),
        name=name,
    )(*args)


def _rope(v, cos, sin_signed, lane):
    swapped = jnp.where(lane < MLA_ROPE // 2,
                        pltpu.roll(v, LANES - MLA_ROPE // 2, axis=1),
                        pltpu.roll(v, MLA_ROPE // 2, axis=1))
    return v * cos + swapped * sin_signed


def _mla_prep_kernel(cq_ref, ckv_ref, tail_ref, pos_ref, freq_ref, qnw_ref, kvnw_ref,
                     wuq_ref, wukv_ref, q_ref, k_ref, v_ref):
    rows = cq_ref.shape[0]
    half = MLA_ROPE // 2
    n_q = LANES // half
    sub = rows // n_q
    ang = pos_ref[...] * freq_ref[...]
    cos_d, sin_d = jnp.cos(ang), jnp.sin(ang)
    lane_s = lax.broadcasted_iota(jnp.int32, (sub, LANES), 1)
    cos_parts, sin_parts = [], []
    for j in range(n_q):
        def spread(t, shift0=(LANES - half * j) % LANES, shift1=(LANES - half * j + half) % LANES):
            lo = pltpu.roll(t, shift0, axis=1) if shift0 else t
            hi = pltpu.roll(t, shift1, axis=1) if shift1 else t
            return lo, hi
        c_lo, c_hi = spread(cos_d)
        s_lo, s_hi = spread(sin_d)
        cos_parts.append(jnp.where(lane_s < half, c_lo, c_hi))
        sin_parts.append(jnp.where(lane_s < half, -s_lo, jnp.where(lane_s < 2 * half, s_hi, 0.0)))
    cos = jnp.concatenate(cos_parts, axis=0)
    sin_signed = jnp.concatenate(sin_parts, axis=0)
    lane = lax.broadcasted_iota(jnp.int32, (rows, LANES), 1)
    scale = (MLA_NOPE + MLA_ROPE) ** -0.5 * LOG2E

    cqn = _rms(cq_ref[...].astype(F32), qnw_ref[...]).astype(BF16)
    q = jnp.dot(cqn, wuq_ref[...], preferred_element_type=F32) * scale
    ckvn = _rms(ckv_ref[...].astype(F32), kvnw_ref[...]).astype(BF16)
    kv = jnp.dot(ckvn, wukv_ref[...], preferred_element_type=F32)
    k_rope = jnp.where(lane < MLA_ROPE, _rope(tail_ref[...], cos, sin_signed, lane), 0.0)
    k_rope = k_rope.astype(BF16)
    for h in range(MLA_HEADS):
        c0 = h * MLA_QK_PAD
        q_ref[:, c0:c0 + MLA_NOPE] = q[:, c0:c0 + MLA_NOPE].astype(BF16)
        q_ref[:, c0 + MLA_NOPE:c0 + MLA_QK_PAD] = _rope(
            q[:, c0 + MLA_NOPE:c0 + MLA_QK_PAD], cos, sin_signed, lane).astype(BF16)
        k_ref[:, c0:c0 + MLA_NOPE] = kv[:, h * MLA_NOPE:(h + 1) * MLA_NOPE].astype(BF16)
        k_ref[:, c0 + MLA_NOPE:c0 + MLA_QK_PAD] = k_rope
    v_ref[...] = kv[:, MLA_HEADS * MLA_NOPE:].astype(BF16)


def _mla_prep(main, tail, pos_d, freq_row, q_norm_w, kv_norm_w, w_uq, w_ukv,
              *, cq_col, ckv_col):
    t = main.shape[0]
    rows = PREP_ROWS
    pos_rows = rows * pos_d.shape[0] // t
    const = lambda shape: pl.BlockSpec(shape, lambda i: (0, 0))
    return pl.pallas_call(
        _mla_prep_kernel,
        grid=(t // rows,),
        in_specs=[
            pl.BlockSpec((rows, MLA_Q_RANK), lambda i: (i, cq_col // MLA_Q_RANK)),
            pl.BlockSpec((rows, MLA_KV_RANK), lambda i: (i, ckv_col // MLA_KV_RANK)),
            pl.BlockSpec((rows, LANES), lambda i: (i, 0)),
            pl.BlockSpec((pos_rows, LANES), lambda i: (i, 0)),
            const((1, LANES)),
            const((1, MLA_Q_RANK)), const((1, MLA_KV_RANK)),
            const(w_uq.shape), const(w_ukv.shape),
        ],
        out_specs=[
            pl.BlockSpec((rows, MLA_HEADS * MLA_QK_PAD), lambda i: (i, 0)),
            pl.BlockSpec((rows, MLA_HEADS * MLA_QK_PAD), lambda i: (i, 0)),
            pl.BlockSpec((rows, MLA_WIDTH), lambda i: (i, 0)),
        ],
        out_shape=[
            jax.ShapeDtypeStruct((t, MLA_HEADS * MLA_QK_PAD), BF16),
            jax.ShapeDtypeStruct((t, MLA_HEADS * MLA_QK_PAD), BF16),
            jax.ShapeDtypeStruct((t, MLA_WIDTH), BF16),
        ],
        compiler_params=_params("parallel"),
        name="mla_prep",
    )(main, main, tail, pos_d, freq_row, q_norm_w.reshape(1, -1),
      kv_norm_w.reshape(1, -1), w_uq, w_ukv)


def _gla_kernel(q_ref, k_ref, v_ref, gg_ref, ga_ref, wa_ref, ba_ref, nw_ref, o_ref,
                hl_ref, qd_ref, ki_ref, ke_ref, qx_ref, qb_ref, kb_ref, dec_ref, att_ref,
                oi_ref, kv_ref, sall_ref):
    seq, dk = q_ref.shape[1], q_ref.shape[2]
    rows = GLA_BLOCK_ROWS
    n_c = rows // CHUNK
    n_x = sum(rows - d * CHUNK for d in range(1, n_c))
    shift = CHUNK.bit_length() - 1
    row = lax.broadcasted_iota(jnp.int32, (rows, rows), 0)
    col = lax.broadcasted_iota(jnp.int32, (rows, rows), 1)
    chunk_gap = (row >> shift) - (col >> shift)
    causal = jnp.logical_and(chunk_gap == 0, row >= col)
    tri = causal.astype(BF16)
    ones = (chunk_gap == 0).astype(BF16)
    nt = (((1,), (1,)), ((), ()))
    tn = (((0,), (0,)), ((), ()))
    blocks = list(range(seq // rows))

    def rows_of(b, lo=0, hi=rows):
        return slice(b * rows + lo, b * rows + hi)

    for b in blocks:
        r = rows_of(b)
        z = jnp.dot(ga_ref[0, r, :].astype(BF16), wa_ref[...], preferred_element_type=F32)
        log_a = _log_sigmoid(z + ba_ref[...]) * (LOG2E / GLA_TAU)
        hi = log_a.astype(BF16)
        hl_ref[r, :dk] = hi
        hl_ref[r, dk:] = (log_a - hi.astype(F32)).astype(BF16)

    for b in blocks:
        r = rows_of(b)
        hl = hl_ref[r, :]
        cum2 = jnp.dot(tri, hl, preferred_element_type=F32)
        tot2 = jnp.dot(ones, hl, preferred_element_type=F32)
        zero = jnp.zeros((1, dk), F32)
        t = [tot2[c * CHUNK:c * CHUNK + 1, :dk] + tot2[c * CHUNK:c * CHUNK + 1, dk:]
             for c in range(n_c)]
        x_base = [b * n_x + sum(rows - e * CHUNK for e in range(1, d)) for d in range(n_c)]
        for c in range(n_c):
            rc = slice(c * CHUNK, (c + 1) * CHUNK)
            rg = rows_of(b, c * CHUNK, (c + 1) * CHUNK)
            cum = cum2[rc, :dk] + cum2[rc, dk:]
            q_dec = q_ref[0, rg, :].astype(F32) * jnp.exp2(cum)
            k = k_ref[0, rg, :].astype(F32)
            k_end = k * jnp.exp2(t[c] - cum)
            qd_ref[rg, :] = q_dec.astype(BF16)
            ki_ref[rg, :] = (k * jnp.exp2(-cum)).astype(BF16)
            ke_ref[rg, :] = k_end.astype(BF16)
            for d in range(1, c + 1):
                gap = sum(t[c - d + 1:c], zero)
                qx_ref[x_base[d] + (c - d) * CHUNK:x_base[d] + (c - d + 1) * CHUNK, :] = (
                    q_dec * jnp.exp2(gap)).astype(BF16)
            qb_ref[rg, :] = (q_dec * jnp.exp2(sum(t[:c], zero))).astype(BF16)
            kb_ref[rg, :] = (k_end * jnp.exp2(sum(t[c + 1:], zero))).astype(BF16)
        dec_ref[b] = jnp.broadcast_to(jnp.exp2(sum(t[1:], t[0])), dec_ref.shape[1:])

    for b in blocks:
        r = rows_of(b)
        same = lax.dot_general(qd_ref[r, :], ki_ref[r, :], nt, preferred_element_type=F32)
        cross = lax.dot_general(qx_ref[b * n_x:(b + 1) * n_x, :], ke_ref[r, :], nt,
                                preferred_element_type=F32)
        x_base = [sum(rows - e * CHUNK for e in range(1, d)) for d in range(n_c)]
        for c in range(n_c):
            rc = slice(c * CHUNK, (c + 1) * CHUNK)
            att = jnp.where(causal[rc], same[rc], 0.0)
            for d in range(1, c + 1):
                x0 = x_base[d] + (c - d) * CHUNK
                att = jnp.where(chunk_gap[rc] == d, cross[x0:x0 + CHUNK], att)
            att_ref[rows_of(b, c * CHUNK, (c + 1) * CHUNK), :] = att.astype(BF16)

    for b in blocks:
        r = rows_of(b)
        v = v_ref[0, r, :]
        oi_ref[r, :] = jnp.dot(att_ref[r, :], v, preferred_element_type=F32)
        kv_ref[b] = lax.dot_general(v, kb_ref[r, :], tn, preferred_element_type=F32)

    state = jnp.zeros(kv_ref.shape[1:], F32)
    for b in blocks:
        sall_ref[b] = state.astype(BF16)
        if b + 1 < len(blocks):
            state = state * dec_ref[b, 0:1, :] + kv_ref[b]

    def inter(b):
        return lax.dot_general(qb_ref[rows_of(b), :], sall_ref[b], nt,
                               preferred_element_type=F32)

    o_inter = inter(0)
    for b in blocks:
        o_next = inter(b + 1) if b + 1 < len(blocks) else None
        for c in range(n_c):
            rc = rows_of(b, c * CHUNK, (c + 1) * CHUNK)
            o = oi_ref[rc, :] + o_inter[c * CHUNK:(c + 1) * CHUNK]
            g = gg_ref[0, rc, :].astype(F32)
            o_ref[0, rc, :] = (_rms(o, nw_ref[...]) * _silu(g)).astype(o_ref.dtype)
        o_inter = o_next


def _gla(main, tail, w_a2, b_a, norm_w):
    b, s, _ = main.shape
    n_blocks = s // GLA_BLOCK_ROWS
    n_cross = sum(GLA_BLOCK_ROWS - d * CHUNK for d in range(1, GLA_BLOCK_ROWS // CHUNK))
    return pl.pallas_call(
        _gla_kernel,
        grid=(b, GLA_HEADS),
        in_specs=[
            pl.BlockSpec((1, s, GLA_DK), lambda i, h: (i, 0, h)),
            pl.BlockSpec((1, s, GLA_DK), lambda i, h: (i, 0, GLA_HEADS + h)),
            pl.BlockSpec((1, s, GLA_DV), lambda i, h: (i, 0, GLA_HEADS + h)),
            pl.BlockSpec((1, s, GLA_DV), lambda i, h: (i, 0, 2 * GLA_HEADS + h)),
            pl.BlockSpec((1, s, LANES), lambda i, h: (i, 0, 0)),
            pl.BlockSpec((LANES, GLA_DK), lambda i, h: (0, h)),
            pl.BlockSpec((1, GLA_DK), lambda i, h: (0, h)),
            pl.BlockSpec((1, GLA_DV), lambda i, h: (0, 0)),
        ],
        out_specs=pl.BlockSpec((1, s, GLA_DV), lambda i, h: (i, 0, h)),
        out_shape=jax.ShapeDtypeStruct((b, s, GLA_VAL_WIDTH), BF16),
        scratch_shapes=[pltpu.VMEM((s, 2 * GLA_DK), BF16),
                        pltpu.VMEM((s, GLA_DK), BF16),
                        pltpu.VMEM((s, GLA_DK), BF16),
                        pltpu.VMEM((s, GLA_DK), BF16),
                        pltpu.VMEM((n_blocks * n_cross, GLA_DK), BF16),
                        pltpu.VMEM((s, GLA_DK), BF16),
                        pltpu.VMEM((s, GLA_DK), BF16),
                        pltpu.VMEM((n_blocks, 8, GLA_DK), F32),
                        pltpu.VMEM((s, GLA_BLOCK_ROWS), BF16),
                        pltpu.VMEM((s, GLA_DV), F32),
                        pltpu.VMEM((n_blocks, GLA_DV, GLA_DK), F32),
                        pltpu.VMEM((n_blocks, GLA_DV, GLA_DK), BF16)],
        compiler_params=_params("parallel", "arbitrary"),
        name="gla",
    )(main, main, main, main, tail, w_a2, b_a.reshape(1, -1), norm_w.reshape(1, -1))


def _out_proj_kernel(*refs, n_parts, keep_residual):
    parts = refs[:n_parts]
    x_ref, w_ref, nw_ref = refs[n_parts:n_parts + 3]
    outs = refs[n_parts + 3:]
    for s0 in range(0, x_ref.shape[0], OUT_SUB_ROWS):
        rs = slice(s0, s0 + OUT_SUB_ROWS)
        y = x_ref[rs, :]
        r0 = 0
        for p in parts:
            width = p.shape[1]
            y = y + jnp.dot(p[rs, :], w_ref[r0:r0 + width, :], preferred_element_type=F32)
            r0 += width
        if keep_residual:
            y_ref, h_ref = outs
            y_ref[rs, :] = y
        else:
            (h_ref,) = outs
        h_ref[rs, :] = _rms(y, nw_ref[...]).astype(h_ref.dtype)


def _out_proj(parts, x, w, norm_w, *, keep_residual):
    t, d = x.shape
    rows = OUT_ROWS
    in_specs = [pl.BlockSpec((rows, p.shape[1]), lambda i: (i, 0)) for p in parts]
    in_specs += [pl.BlockSpec((rows, d), lambda i: (i, 0)),
                 pl.BlockSpec(w.shape, lambda i: (0, 0), pipeline_mode=pl.Buffered(1)),
                 pl.BlockSpec((1, d), lambda i: (0, 0))]
    args = list(parts) + [x, w, norm_w.reshape(1, d)]
    row_spec = pl.BlockSpec((rows, d), lambda i: (i, 0))
    if keep_residual:
        out_specs = [row_spec, row_spec]
        out_shape = [jax.ShapeDtypeStruct((t, d), F32), jax.ShapeDtypeStruct((t, d), BF16)]
    else:
        out_specs = row_spec
        out_shape = jax.ShapeDtypeStruct((t, d), F32)
    return pl.pallas_call(
        functools.partial(_out_proj_kernel, n_parts=len(parts), keep_residual=keep_residual),
        grid=(t // rows,),
        in_specs=in_specs,
        out_specs=out_specs,
        out_shape=out_shape,
        compiler_params=_params("parallel"),
        name="out_proj",
    )(*args)


def _memory_branch(main, mem2d, mem_norm_w, w_mem_kv, *, batch, q_col, g_col):
    n = w_mem_kv.shape[1]
    mem_kv = _norm_proj(mem2d, mem_norm_w, [w_mem_kv.astype(BF16)], jnp.ones((1, n), F32), None,
                        rows=PROJ_ROWS)
    mem_kv = mem_kv.reshape(batch, -1, n)
    return _attention(main, mem_kv, mem_kv, main, None, heads=MEM_HEADS, dq=MEM_HEAD_DIM,
                      dv=MEM_HEAD_DIM, q_blk=q_col // MEM_HEAD_DIM, k_blk=0, v_blk=MEM_HEADS,
                      g_blk=g_col // MEM_HEAD_DIM, mask=None, q_rows=ATTN_Q_ROWS,
                      name="mem_attn")


def _col_scale(widths_and_scales):
    return jnp.concatenate([jnp.full((1, w), s, F32) for w, s in widths_and_scales], axis=1)


def _split_cols(w, sizes):
    out, acc = [], 0
    for s in sizes:
        out.append(w[:, acc:acc + s].astype(BF16))
        acc += s
    return out


def _even_layer(x2d, mem2d, pos_d, freq_row, batch, norm_w, w_in, fox_b_f,
                mla_q_norm_w, mla_w_uq, mla_kv_norm_w, mla_w_ukv, mem_norm_w, w_mem_kv, w_out,
                next_norm_w):
    t = x2d.shape[0]
    seq = t // batch
    (w_a, w_ff, w_b, w_kr, w_c) = _split_cols(
        w_in, (3 * FOX_WIDTH, FOX_HEADS, FOX_WIDTH + MLA_Q_RANK + MLA_KV_RANK,
               MLA_ROPE, MLA_WIDTH + 2 * MEM_WIDTH))
    fq_col, fk_col, fv_col, fg_col = 0, FOX_WIDTH, 2 * FOX_WIDTH, 3 * FOX_WIDTH
    cq_col = 4 * FOX_WIDTH
    ckv_col = cq_col + MLA_Q_RANK
    mg_col = ckv_col + MLA_KV_RANK
    mq_col = mg_col + MLA_WIDTH
    memg_col = mq_col + MEM_WIDTH
    n_main = memg_col + MEM_WIDTH
    ff_lane = MLA_ROPE
    w_tail = jnp.concatenate(
        [w_kr, w_ff, jnp.zeros((D_MODEL, LANES - MLA_ROPE - FOX_HEADS), BF16)], axis=1)
    col_scale = _col_scale([(FOX_WIDTH, FOX_HEAD_DIM ** -0.5 * LOG2E), (mq_col - FOX_WIDTH, 1.0),
                            (MEM_WIDTH, MEM_HEAD_DIM ** -0.5 * LOG2E), (MEM_WIDTH, 1.0)])
    main, tail = _norm_proj(x2d, norm_w, [w_a, w_b, w_c], col_scale, w_tail, rows=PROJ_ROWS)
    main3 = main.reshape(batch, seq, n_main)

    bias_row = jnp.zeros((1, LANES), F32).at[0, ff_lane:ff_lane + FOX_HEADS].set(fox_b_f)
    neg_c = _fox_gate(tail.reshape(batch, seq, LANES), bias_row, lane0=ff_lane, heads=FOX_HEADS)
    o_a = _attention(main3, main3, main3, main3, neg_c.reshape(batch * FOX_HEADS, 1, seq),
                     heads=FOX_HEADS, dq=FOX_HEAD_DIM, dv=FOX_HEAD_DIM,
                     q_blk=fq_col // FOX_HEAD_DIM, k_blk=fk_col // FOX_HEAD_DIM,
                     v_blk=fv_col // FOX_HEAD_DIM, g_blk=fg_col // FOX_HEAD_DIM,
                     mask="causal", q_rows=ATTN_Q_ROWS, name="fox_attn")

    pad = jnp.zeros((MLA_Q_RANK, MLA_HEADS, MLA_QK_PAD - MLA_NOPE - MLA_ROPE), F32)
    w_uq = jnp.concatenate([mla_w_uq.reshape(MLA_Q_RANK, MLA_HEADS, MLA_NOPE + MLA_ROPE), pad],
                           axis=2).reshape(MLA_Q_RANK, MLA_HEADS * MLA_QK_PAD).astype(BF16)
    w_ukv = mla_w_ukv.reshape(MLA_KV_RANK, MLA_HEADS, MLA_NOPE + MLA_V)
    w_ukv = jnp.concatenate([w_ukv[:, :, :MLA_NOPE].reshape(MLA_KV_RANK, -1),
                             w_ukv[:, :, MLA_NOPE:].reshape(MLA_KV_RANK, -1)], axis=1).astype(BF16)
    q_cat, k_cat, v_b = _mla_prep(main, tail, pos_d, freq_row, mla_q_norm_w,
                                  mla_kv_norm_w, w_uq, w_ukv, cq_col=cq_col, ckv_col=ckv_col)
    o_b = _attention(q_cat.reshape(batch, seq, -1), k_cat.reshape(batch, seq, -1),
                     v_b.reshape(batch, seq, -1), main3, None, heads=MLA_HEADS, dq=MLA_QK_PAD,
                     dv=MLA_V, q_blk=0, k_blk=0, v_blk=0, g_blk=mg_col // MLA_V,
                     mask="chunk", q_rows=ATTN_Q_ROWS, name="mla_attn")

    o_m = _memory_branch(main3, mem2d, mem_norm_w, w_mem_kv, batch=batch, q_col=mq_col,
                         g_col=memg_col)
    parts = [o_a.reshape(t, -1), o_b.reshape(t, -1), o_m.reshape(t, -1)]
    return _out_proj(parts, x2d, w_out.astype(BF16), next_norm_w, keep_residual=True)


def _odd_layer(x2d, h2d, mem2d, batch, w_in, gla_w_a2, gla_b_a, gla_norm_w, mem_norm_w,
               w_mem_kv, w_out, final_w):
    t = x2d.shape[0]
    seq = t // batch
    (w_a, w_ga, w_b) = _split_cols(
        w_in, (2 * GLA_KEY_WIDTH + GLA_VAL_WIDTH, GLA_GATE_RANK,
               GLA_VAL_WIDTH + 2 * MEM_WIDTH))
    mq_col = 2 * GLA_KEY_WIDTH + 2 * GLA_VAL_WIDTH
    memg_col = mq_col + MEM_WIDTH
    n_main = memg_col + MEM_WIDTH
    w_tail = jnp.concatenate([w_ga, jnp.zeros((D_MODEL, LANES - GLA_GATE_RANK), BF16)], axis=1)
    col_scale = _col_scale([(GLA_KEY_WIDTH, GLA_DK ** -0.5), (mq_col - GLA_KEY_WIDTH, 1.0),
                            (MEM_WIDTH, MEM_HEAD_DIM ** -0.5 * LOG2E), (MEM_WIDTH, 1.0)])
    main, tail = _norm_proj(h2d, None, [w_a, w_b], col_scale, w_tail, rows=PROJ_ROWS)
    main3 = main.reshape(batch, seq, n_main)

    w_a2 = jnp.concatenate([gla_w_a2, jnp.zeros((LANES - GLA_GATE_RANK, GLA_KEY_WIDTH), F32)],
                           axis=0).astype(BF16)
    o_c = _gla(main3, tail.reshape(batch, seq, LANES), w_a2, gla_b_a, gla_norm_w)
    o_m = _memory_branch(main3, mem2d, mem_norm_w, w_mem_kv, batch=batch, q_col=mq_col,
                         g_col=memg_col)
    parts = [o_c.reshape(t, -1), o_m.reshape(t, -1)]
    return _out_proj(parts, x2d, w_out.astype(BF16), final_w, keep_residual=False)


def kernel(x, mem, positions, e_norm_w, e_w_in, e_fox_b_f, e_mla_q_norm_w, e_mla_w_uq, e_mla_kv_norm_w, e_mla_w_ukv, e_mem_norm_w, e_w_mem_kv, e_w_out, o_norm_w, o_w_in, o_gla_w_a2, o_gla_b_a, o_gla_norm_w, o_mem_norm_w, o_w_mem_kv, o_w_out, final_norm_w):
    batch, seq, d = x.shape
    t = batch * seq
    x2d = x.reshape(t, d)
    mem2d = mem.reshape(-1, d)
    inv_freq = 1.0 / (ROPE_THETA ** (jnp.arange(0, MLA_ROPE, 2, dtype=F32) / MLA_ROPE))
    half = MLA_ROPE // 2
    n_q = LANES // half
    freq_row = jnp.tile(inv_freq, n_q).reshape(1, LANES)
    pos_d = positions.astype(F32).reshape(t // PREP_ROWS, n_q, PREP_ROWS // n_q)
    pos_d = jnp.repeat(pos_d.transpose(0, 2, 1), half, axis=2).reshape(t // n_q, LANES)

    x2d, h2d = _even_layer(x2d, mem2d, pos_d, freq_row, batch, e_norm_w[0], e_w_in[0],
                           e_fox_b_f[0], e_mla_q_norm_w[0], e_mla_w_uq[0], e_mla_kv_norm_w[0],
                           e_mla_w_ukv[0], e_mem_norm_w[0], e_w_mem_kv[0], e_w_out[0],
                           o_norm_w[0])
    out = _odd_layer(x2d, h2d, mem2d, batch, o_w_in[0], o_gla_w_a2[0], o_gla_b_a[0],
                     o_gla_norm_w[0], o_mem_norm_w[0], o_w_mem_kv[0], o_w_out[0], final_norm_w)
    return out.reshape(batch, seq, d)
```

```python
import functools

import jax
import jax.numpy as jnp
from jax import lax
from jax.experimental import pallas as pl
from jax.experimental.pallas import tpu as pltpu

F32 = jnp.float32
BF16 = jnp.bfloat16

D_MODEL = 2048
CHUNK = 64
EPS = 1e-6
NEG = -1e30

FOX_HEADS = 8
FOX_HEAD_DIM = 128
FOX_WIDTH = FOX_HEADS * FOX_HEAD_DIM

MLA_HEADS = 8
MLA_NOPE = 128
MLA_ROPE = 64
MLA_V = 128
MLA_Q_RANK = 512
MLA_KV_RANK = 256
MLA_WIDTH = MLA_HEADS * MLA_V
MLA_QK_PAD = 256
ROPE_THETA = 10000.0

GLA_HEADS = 4
GLA_KEY_WIDTH = D_MODEL // 2
GLA_VAL_WIDTH = D_MODEL
GLA_DK = GLA_KEY_WIDTH // GLA_HEADS
GLA_DV = GLA_VAL_WIDTH // GLA_HEADS
GLA_GATE_RANK = 16
GLA_TAU = 16.0

MEM_HEADS = 4
MEM_HEAD_DIM = 128
MEM_WIDTH = MEM_HEADS * MEM_HEAD_DIM

LANES = 128
VMEM_LIMIT_BYTES = 56 * 1024 * 1024
PROJ_ROWS = 256
PROJ_COLS = 256
OUT_ROWS = 512
OUT_SUB_ROWS = 256
PREP_ROWS = 512
ATTN_Q_ROWS = 256
ATTN_K_COLS = 256
ATTN_MAX_HEADS_PER_STEP = 4
LOG2E = 1.4426950408889634
GLA_BLOCK_ROWS = 256


def _params(*sem):
    return pltpu.CompilerParams(dimension_semantics=sem, vmem_limit_bytes=VMEM_LIMIT_BYTES)


def _rms(x, w):
    return x * lax.rsqrt(jnp.mean(x * x, axis=-1, keepdims=True) + EPS) * w


def _log_sigmoid(z):
    return jnp.minimum(z, 0.0) - jnp.log(1.0 + jnp.exp(-jnp.abs(z)))


def _silu(g):
    h = 0.5 * g
    return h + h * jnp.tanh(h)


def _norm_proj_kernel(x_ref, *rest, n_groups, with_norm, with_tail):
    rest = list(rest)
    nw_ref = rest.pop(0) if with_norm else None
    w_refs = [rest.pop(0) for _ in range(n_groups)]
    cs_ref = rest.pop(0)
    wt_ref = rest.pop(0) if with_tail else None
    o_ref = rest.pop(0)
    ot_ref = rest.pop(0) if with_tail else None
    if with_norm:
        (h_ref,) = rest
        h_ref[...] = _rms(x_ref[...], nw_ref[...]).astype(BF16)
    else:
        h_ref = x_ref
    if with_tail:
        ot_ref[...] = jnp.dot(h_ref[...], wt_ref[...], preferred_element_type=F32)
    c0 = 0
    for w_ref in w_refs:
        for c in range(0, w_ref.shape[1], PROJ_COLS):
            acc = jnp.dot(h_ref[...], w_ref[:, c:c + PROJ_COLS], preferred_element_type=F32)
            dst = slice(c0 + c, c0 + c + PROJ_COLS)
            o_ref[:, dst] = (acc * cs_ref[:, dst]).astype(o_ref.dtype)
        c0 += w_ref.shape[1]


def _norm_proj(x, norm_w, ws, col_scale, w_tail, *, rows):
    t, k = x.shape
    n = sum(w.shape[1] for w in ws)
    with_norm = norm_w is not None
    with_tail = w_tail is not None

    def resident(shape):
        return pl.BlockSpec(shape, lambda i: (0, 0), pipeline_mode=pl.Buffered(1))

    in_specs = [pl.BlockSpec((rows, k), lambda i: (i, 0))]
    args = [x]
    if with_norm:
        in_specs.append(resident((1, k)))
        args.append(norm_w.reshape(1, k))
    in_specs += [resident(w.shape) for w in ws] + [resident((1, n))]
    args += list(ws) + [col_scale]
    out_shape = [jax.ShapeDtypeStruct((t, n), BF16)]
    out_specs = [pl.BlockSpec((rows, n), lambda i: (i, 0))]
    if with_tail:
        in_specs.append(resident((k, LANES)))
        args.append(w_tail)
        out_shape.append(jax.ShapeDtypeStruct((t, LANES), F32))
        out_specs.append(pl.BlockSpec((rows, LANES), lambda i: (i, 0)))
    outs = pl.pallas_call(
        functools.partial(_norm_proj_kernel, n_groups=len(ws), with_norm=with_norm,
                          with_tail=with_tail),
        grid=(t // rows,),
        in_specs=in_specs,
        out_specs=out_specs,
        out_shape=out_shape,
        scratch_shapes=[pltpu.VMEM((rows, k), BF16)] if with_norm else [],
        compiler_params=_params("parallel"),
        name="norm_proj",
    )(*args)
    return outs if with_tail else outs[0]


def _fox_gate_kernel(tail_ref, bias_ref, o_ref, *, lane0, heads):
    seq = tail_ref.shape[1]
    logf = _log_sigmoid(tail_ref[0] + bias_ref[...])
    c = logf.T[lane0:lane0 + heads, :]
    lane = lax.broadcasted_iota(jnp.int32, c.shape, 1)
    shift = 1
    while shift < seq:
        c = c + jnp.where(lane >= shift, pltpu.roll(c, shift, axis=1), 0.0)
        shift *= 2
    o_ref[0] = c * -LOG2E


def _fox_gate(tail, bias_row, *, lane0, heads):
    b, s, _ = tail.shape
    return pl.pallas_call(
        functools.partial(_fox_gate_kernel, lane0=lane0, heads=heads),
        grid=(b,),
        in_specs=[pl.BlockSpec((1, s, LANES), lambda i: (i, 0, 0)),
                  pl.BlockSpec((1, LANES), lambda i: (0, 0))],
        out_specs=pl.BlockSpec((1, heads, s), lambda i: (i, 0, 0)),
        out_shape=jax.ShapeDtypeStruct((b, heads, s), F32),
        compiler_params=_params("parallel"),
        name="fox_gate",
    )(tail, bias_row)


def _attn_kernel(q_ref, k_ref, v_ref, g_ref, *rest, mask, q_rows, with_bias, hps):
    if with_bias:
        b_ref, o_ref, s_ref, v1_ref = rest
    else:
        o_ref, s_ref, v1_ref = rest
    seq, sk = q_ref.shape[1], k_ref.shape[1]
    dq, dv = q_ref.shape[2] // hps, v_ref.shape[2] // hps
    for hh in range(hps):
        v1_ref[hh, :, :dv] = v_ref[0, :, hh * dv:(hh + 1) * dv]
        v1_ref[hh, :, dv:] = jnp.ones((sk, dv), BF16)
    k_cols = ATTN_K_COLS
    nt = (((1,), (1,)), ((), ()))

    def lane_slabs(t):
        return [t[:, c:c + LANES] for c in range(0, t.shape[1], LANES)]

    tasks = [(hh, i) for hh in range(hps) for i in range(seq // q_rows)]

    def tiles(i):
        return range(0, sk if mask is None else (i + 1) * q_rows, k_cols)

    def pass1(n):
        hh, i = tasks[n]
        lo = i * q_rows
        q = q_ref[0, lo:lo + q_rows, hh * dq:(hh + 1) * dq]
        mx = jnp.full((q_rows, LANES), NEG, F32)
        for a in tiles(i):
            s = lax.dot_general(q, k_ref[0, a:a + k_cols, hh * dq:(hh + 1) * dq], nt,
                                preferred_element_type=F32)
            if with_bias:
                s = s + b_ref[hh, :, a:a + k_cols]
            if mask is not None and a + k_cols > lo + 1:
                row = lo + lax.broadcasted_iota(jnp.int32, s.shape, 0)
                col = a + lax.broadcasted_iota(jnp.int32, s.shape, 1)
                if mask == "chunk":
                    shift = CHUNK.bit_length() - 1
                    row, col = row >> shift, col >> shift
                s = jnp.where(row >= col, s, NEG)
            s_ref[n % 2, :, a:a + k_cols] = s
            mx = functools.reduce(jnp.maximum, lane_slabs(s), mx)
        return jnp.max(mx, axis=-1, keepdims=True)

    def pass2(n, m):
        hh, i = tasks[n]
        lo = i * q_rows
        o = jnp.zeros((q_rows, 2 * dv), F32)
        for a in tiles(i):
            p = jnp.exp2(s_ref[n % 2, :, a:a + k_cols] - m)
            o = o + jnp.dot(p.astype(BF16), v1_ref[hh, a:a + k_cols, :],
                            preferred_element_type=F32)
        g = g_ref[0, lo:lo + q_rows, hh * dv:(hh + 1) * dv].astype(F32)
        o_ref[0, lo:lo + q_rows, hh * dv:(hh + 1) * dv] = (
            o[:, :dv] / o[:, dv:] * _silu(g)).astype(o_ref.dtype)

    m = pass1(0)
    for n in range(len(tasks)):
        m_next = pass1(n + 1) if n + 1 < len(tasks) else None
        pass2(n, m)
        m = m_next


def _attention(q, k, v, gate, bias, *, heads, dq, dv, q_blk, k_blk, v_blk, g_blk,
               mask, q_rows, name):
    b, s, _ = q.shape
    sk = k.shape[1]
    hps = next(h for h in (ATTN_MAX_HEADS_PER_STEP, 2, 1)
               if all(n % h == 0 for n in (heads, q_blk, k_blk, v_blk, g_blk)))
    groups = heads // hps
    in_specs = [
        pl.BlockSpec((1, s, hps * dq), lambda i, h: (i, 0, q_blk // hps + h)),
        pl.BlockSpec((1, sk, hps * dq), lambda i, h: (i, 0, k_blk // hps + h)),
        pl.BlockSpec((1, sk, hps * dv), lambda i, h: (i, 0, v_blk // hps + h)),
        pl.BlockSpec((1, s, hps * dv), lambda i, h: (i, 0, g_blk // hps + h)),
    ]
    args = [q, k, v, gate]
    if bias is not None:
        in_specs.append(pl.BlockSpec((hps, 1, sk), lambda i, h: (i * groups + h, 0, 0)))
        args.append(bias)
    return pl.pallas_call(
        functools.partial(_attn_kernel, mask=mask, q_rows=q_rows, with_bias=bias is not None,
                          hps=hps),
        grid=(b, groups),
        in_specs=in_specs,
        out_specs=pl.BlockSpec((1, s, hps * dv), lambda i, h: (i, 0, h)),
        out_shape=jax.ShapeDtypeStruct((b, s, heads * dv), BF16),
        scratch_shapes=[pltpu.VMEM((2, q_rows, sk), F32),
                        pltpu.VMEM((hps, sk, 2 * dv), BF16)],
        compiler_params=_params("parallel", "arbitrary"),
        name=name,
    )(*args)


def _rope(v, cos, sin_signed, lane):
    swapped = jnp.where(lane < MLA_ROPE // 2,
                        pltpu.roll(v, LANES - MLA_ROPE // 2, axis=1),
                        pltpu.roll(v, MLA_ROPE // 2, axis=1))
    return v * cos + swapped * sin_signed


def _mla_prep_kernel(cq_ref, ckv_ref, tail_ref, pos_ref, freq_ref, qnw_ref, kvnw_ref,
                     wuq_ref, wukv_ref, q_ref, k_ref, v_ref):
    rows = cq_ref.shape[0]
    half = MLA_ROPE // 2
    n_q = LANES // half
    sub = rows // n_q
    ang = pos_ref[...] * freq_ref[...]
    cos_d, sin_d = jnp.cos(ang), jnp.sin(ang)
    lane_s = lax.broadcasted_iota(jnp.int32, (sub, LANES), 1)
    cos_parts, sin_parts = [], []
    for j in range(n_q):
        def spread(t, shift0=(LANES - half * j) % LANES, shift1=(LANES - half * j + half) % LANES):
            lo = pltpu.roll(t, shift0, axis=1) if shift0 else t
            hi = pltpu.roll(t, shift1, axis=1) if shift1 else t
            return lo, hi
        c_lo, c_hi = spread(cos_d)
        s_lo, s_hi = spread(sin_d)
        cos_parts.append(jnp.where(lane_s < half, c_lo, c_hi))
        sin_parts.append(jnp.where(lane_s < half, -s_lo, jnp.where(lane_s < 2 * half, s_hi, 0.0)))
    cos = jnp.concatenate(cos_parts, axis=0)
    sin_signed = jnp.concatenate(sin_parts, axis=0)
    lane = lax.broadcasted_iota(jnp.int32, (rows, LANES), 1)
    scale = (MLA_NOPE + MLA_ROPE) ** -0.5 * LOG2E

    cqn = _rms(cq_ref[...].astype(F32), qnw_ref[...]).astype(BF16)
    q = jnp.dot(cqn, wuq_ref[...], preferred_element_type=F32) * scale
    ckvn = _rms(ckv_ref[...].astype(F32), kvnw_ref[...]).astype(BF16)
    kv = jnp.dot(ckvn, wukv_ref[...], preferred_element_type=F32)
    k_rope = jnp.where(lane < MLA_ROPE, _rope(tail_ref[...], cos, sin_signed, lane), 0.0)
    k_rope = k_rope.astype(BF16)
    for h in range(MLA_HEADS):
        c0 = h * MLA_QK_PAD
        q_ref[:, c0:c0 + MLA_NOPE] = q[:, c0:c0 + MLA_NOPE].astype(BF16)
        q_ref[:, c0 + MLA_NOPE:c0 + MLA_QK_PAD] = _rope(
            q[:, c0 + MLA_NOPE:c0 + MLA_QK_PAD], cos, sin_signed, lane).astype(BF16)
        k_ref[:, c0:c0 + MLA_NOPE] = kv[:, h * MLA_NOPE:(h + 1) * MLA_NOPE].astype(BF16)
        k_ref[:, c0 + MLA_NOPE:c0 + MLA_QK_PAD] = k_rope
    v_ref[...] = kv[:, MLA_HEADS * MLA_NOPE:].astype(BF16)


def _mla_prep(main, tail, pos_d, freq_row, q_norm_w, kv_norm_w, w_uq, w_ukv,
              *, cq_col, ckv_col):
    t = main.shape[0]
    rows = PREP_ROWS
    pos_rows = rows * pos_d.shape[0] // t
    const = lambda shape: pl.BlockSpec(shape, lambda i: (0, 0))
    return pl.pallas_call(
        _mla_prep_kernel,
        grid=(t // rows,),
        in_specs=[
            pl.BlockSpec((rows, MLA_Q_RANK), lambda i: (i, cq_col // MLA_Q_RANK)),
            pl.BlockSpec((rows, MLA_KV_RANK), lambda i: (i, ckv_col // MLA_KV_RANK)),
            pl.BlockSpec((rows, LANES), lambda i: (i, 0)),
            pl.BlockSpec((pos_rows, LANES), lambda i: (i, 0)),
            const((1, LANES)),
            const((1, MLA_Q_RANK)), const((1, MLA_KV_RANK)),
            const(w_uq.shape), const(w_ukv.shape),
        ],
        out_specs=[
            pl.BlockSpec((rows, MLA_HEADS * MLA_QK_PAD), lambda i: (i, 0)),
            pl.BlockSpec((rows, MLA_HEADS * MLA_QK_PAD), lambda i: (i, 0)),
            pl.BlockSpec((rows, MLA_WIDTH), lambda i: (i, 0)),
        ],
        out_shape=[
            jax.ShapeDtypeStruct((t, MLA_HEADS * MLA_QK_PAD), BF16),
            jax.ShapeDtypeStruct((t, MLA_HEADS * MLA_QK_PAD), BF16),
            jax.ShapeDtypeStruct((t, MLA_WIDTH), BF16),
        ],
        compiler_params=_params("parallel"),
        name="mla_prep",
    )(main, main, tail, pos_d, freq_row, q_norm_w.reshape(1, -1),
      kv_norm_w.reshape(1, -1), w_uq, w_ukv)


def _gla_kernel(q_ref, k_ref, v_ref, gg_ref, ga_ref, wa_ref, ba_ref, nw_ref, o_ref,
                hl_ref, qd_ref, ki_ref, ke_ref, qx_ref, qb_ref, kb_ref, dec_ref, att_ref,
                oi_ref, kv_ref, sall_ref):
    seq, dk = q_ref.shape[1], q_ref.shape[2]
    rows = GLA_BLOCK_ROWS
    n_c = rows // CHUNK
    n_x = sum(rows - d * CHUNK for d in range(1, n_c))
    shift = CHUNK.bit_length() - 1
    row = lax.broadcasted_iota(jnp.int32, (rows, rows), 0)
    col = lax.broadcasted_iota(jnp.int32, (rows, rows), 1)
    chunk_gap = (row >> shift) - (col >> shift)
    causal = jnp.logical_and(chunk_gap == 0, row >= col)
    tri = causal.astype(BF16)
    ones = (chunk_gap == 0).astype(BF16)
    nt = (((1,), (1,)), ((), ()))
    tn = (((0,), (0,)), ((), ()))
    blocks = list(range(seq // rows))

    def rows_of(b, lo=0, hi=rows):
        return slice(b * rows + lo, b * rows + hi)

    for b in blocks:
        r = rows_of(b)
        z = jnp.dot(ga_ref[0, r, :].astype(BF16), wa_ref[...], preferred_element_type=F32)
        log_a = _log_sigmoid(z + ba_ref[...]) * (LOG2E / GLA_TAU)
        hi = log_a.astype(BF16)
        hl_ref[r, :dk] = hi
        hl_ref[r, dk:] = (log_a - hi.astype(F32)).astype(BF16)

    for b in blocks:
        r = rows_of(b)
        hl = hl_ref[r, :]
        cum2 = jnp.dot(tri, hl, preferred_element_type=F32)
        tot2 = jnp.dot(ones, hl, preferred_element_type=F32)
        zero = jnp.zeros((1, dk), F32)
        t = [tot2[c * CHUNK:c * CHUNK + 1, :dk] + tot2[c * CHUNK:c * CHUNK + 1, dk:]
             for c in range(n_c)]
        x_base = [b * n_x + sum(rows - e * CHUNK for e in range(1, d)) for d in range(n_c)]
        for c in range(n_c):
            rc = slice(c * CHUNK, (c + 1) * CHUNK)
            rg = rows_of(b, c * CHUNK, (c + 1) * CHUNK)
            cum = cum2[rc, :dk] + cum2[rc, dk:]
            q_dec = q_ref[0, rg, :].astype(F32) * jnp.exp2(cum)
            k = k_ref[0, rg, :].astype(F32)
            k_end = k * jnp.exp2(t[c] - cum)
            qd_ref[rg, :] = q_dec.astype(BF16)
            ki_ref[rg, :] = (k * jnp.exp2(-cum)).astype(BF16)
            ke_ref[rg, :] = k_end.astype(BF16)
            for d in range(1, c + 1):
                gap = sum(t[c - d + 1:c], zero)
                qx_ref[x_base[d] + (c - d) * CHUNK:x_base[d] + (c - d + 1) * CHUNK, :] = (
                    q_dec * jnp.exp2(gap)).astype(BF16)
            qb_ref[rg, :] = (q_dec * jnp.exp2(sum(t[:c], zero))).astype(BF16)
            kb_ref[rg, :] = (k_end * jnp.exp2(sum(t[c + 1:], zero))).astype(BF16)
        dec_ref[b] = jnp.broadcast_to(jnp.exp2(sum(t[1:], t[0])), dec_ref.shape[1:])

    for b in blocks:
        r = rows_of(b)
        same = lax.dot_general(qd_ref[r, :], ki_ref[r, :], nt, preferred_element_type=F32)
        cross = lax.dot_general(qx_ref[b * n_x:(b + 1) * n_x, :], ke_ref[r, :], nt,
                                preferred_element_type=F32)
        x_base = [sum(rows - e * CHUNK for e in range(1, d)) for d in range(n_c)]
        for c in range(n_c):
            rc = slice(c * CHUNK, (c + 1) * CHUNK)
            att = jnp.where(causal[rc], same[rc], 0.0)
            for d in range(1, c + 1):
                x0 = x_base[d] + (c - d) * CHUNK
                att = jnp.where(chunk_gap[rc] == d, cross[x0:x0 + CHUNK], att)
            att_ref[rows_of(b, c * CHUNK, (c + 1) * CHUNK), :] = att.astype(BF16)

    for b in blocks:
        r = rows_of(b)
        v = v_ref[0, r, :]
        oi_ref[r, :] = jnp.dot(att_ref[r, :], v, preferred_element_type=F32)
        kv_ref[b] = lax.dot_general(v, kb_ref[r, :], tn, preferred_element_type=F32)

    state = jnp.zeros(kv_ref.shape[1:], F32)
    for b in blocks:
        sall_ref[b] = state.astype(BF16)
        if b + 1 < len(blocks):
            state = state * dec_ref[b, 0:1, :] + kv_ref[b]

    def inter(b):
        return lax.dot_general(qb_ref[rows_of(b), :], sall_ref[b], nt,
                               preferred_element_type=F32)

    o_inter = inter(0)
    for b in blocks:
        o_next = inter(b + 1) if b + 1 < len(blocks) else None
        for c in range(n_c):
            rc = rows_of(b, c * CHUNK, (c + 1) * CHUNK)
            o = oi_ref[rc, :] + o_inter[c * CHUNK:(c + 1) * CHUNK]
            g = gg_ref[0, rc, :].astype(F32)
            o_ref[0, rc, :] = (_rms(o, nw_ref[...]) * _silu(g)).astype(o_ref.dtype)
        o_inter = o_next


def _gla(main, tail, w_a2, b_a, norm_w):
    b, s, _ = main.shape
    n_blocks = s // GLA_BLOCK_ROWS
    n_cross = sum(GLA_BLOCK_ROWS - d * CHUNK for d in range(1, GLA_BLOCK_ROWS // CHUNK))
    return pl.pallas_call(
        _gla_kernel,
        grid=(b, GLA_HEADS),
        in_specs=[
            pl.BlockSpec((1, s, GLA_DK), lambda i, h: (i, 0, h)),
            pl.BlockSpec((1, s, GLA_DK), lambda i, h: (i, 0, GLA_HEADS + h)),
            pl.BlockSpec((1, s, GLA_DV), lambda i, h: (i, 0, GLA_HEADS + h)),
            pl.BlockSpec((1, s, GLA_DV), lambda i, h: (i, 0, 2 * GLA_HEADS + h)),
            pl.BlockSpec((1, s, LANES), lambda i, h: (i, 0, 0)),
            pl.BlockSpec((LANES, GLA_DK), lambda i, h: (0, h)),
            pl.BlockSpec((1, GLA_DK), lambda i, h: (0, h)),
            pl.BlockSpec((1, GLA_DV), lambda i, h: (0, 0)),
        ],
        out_specs=pl.BlockSpec((1, s, GLA_DV), lambda i, h: (i, 0, h)),
        out_shape=jax.ShapeDtypeStruct((b, s, GLA_VAL_WIDTH), BF16),
        scratch_shapes=[pltpu.VMEM((s, 2 * GLA_DK), BF16),
                        pltpu.VMEM((s, GLA_DK), BF16),
                        pltpu.VMEM((s, GLA_DK), BF16),
                        pltpu.VMEM((s, GLA_DK), BF16),
                        pltpu.VMEM((n_blocks * n_cross, GLA_DK), BF16),
                        pltpu.VMEM((s, GLA_DK), BF16),
                        pltpu.VMEM((s, GLA_DK), BF16),
                        pltpu.VMEM((n_blocks, 8, GLA_DK), F32),
                        pltpu.VMEM((s, GLA_BLOCK_ROWS), BF16),
                        pltpu.VMEM((s, GLA_DV), F32),
                        pltpu.VMEM((n_blocks, GLA_DV, GLA_DK), F32),
                        pltpu.VMEM((n_blocks, GLA_DV, GLA_DK), BF16)],
        compiler_params=_params("parallel", "arbitrary"),
        name="gla",
    )(main, main, main, main, tail, w_a2, b_a.reshape(1, -1), norm_w.reshape(1, -1))


def _out_proj_kernel(*refs, n_parts, keep_residual):
    parts = refs[:n_parts]
    x_ref, w_ref, nw_ref = refs[n_parts:n_parts + 3]
    outs = refs[n_parts + 3:]
    for s0 in range(0, x_ref.shape[0], OUT_SUB_ROWS):
        rs = slice(s0, s0 + OUT_SUB_ROWS)
        y = x_ref[rs, :]
        r0 = 0
        for p in parts:
            width = p.shape[1]
            y = y + jnp.dot(p[rs, :], w_ref[r0:r0 + width, :], preferred_element_type=F32)
            r0 += width
        if keep_residual:
            y_ref, h_ref = outs
            y_ref[rs, :] = y
        else:
            (h_ref,) = outs
        h_ref[rs, :] = _rms(y, nw_ref[...]).astype(h_ref.dtype)


def _out_proj(parts, x, w, norm_w, *, keep_residual):
    t, d = x.shape
    rows = OUT_ROWS
    in_specs = [pl.BlockSpec((rows, p.shape[1]), lambda i: (i, 0)) for p in parts]
    in_specs += [pl.BlockSpec((rows, d), lambda i: (i, 0)),
                 pl.BlockSpec(w.shape, lambda i: (0, 0), pipeline_mode=pl.Buffered(1)),
                 pl.BlockSpec((1, d), lambda i: (0, 0))]
    args = list(parts) + [x, w, norm_w.reshape(1, d)]
    row_spec = pl.BlockSpec((rows, d), lambda i: (i, 0))
    if keep_residual:
        out_specs = [row_spec, row_spec]
        out_shape = [jax.ShapeDtypeStruct((t, d), F32), jax.ShapeDtypeStruct((t, d), BF16)]
    else:
        out_specs = row_spec
        out_shape = jax.ShapeDtypeStruct((t, d), F32)
    return pl.pallas_call(
        functools.partial(_out_proj_kernel, n_parts=len(parts), keep_residual=keep_residual),
        grid=(t // rows,),
        in_specs=in_specs,
        out_specs=out_specs,
        out_shape=out_shape,
        compiler_params=_params("parallel"),
        name="out_proj",
    )(*args)


def _memory_branch(main, mem2d, mem_norm_w, w_mem_kv, *, batch, q_col, g_col):
    n = w_mem_kv.shape[1]
    mem_kv = _norm_proj(mem2d, mem_norm_w, [w_mem_kv.astype(BF16)], jnp.ones((1, n), F32), None,
                        rows=PROJ_ROWS)
    mem_kv = mem_kv.reshape(batch, -1, n)
    return _attention(main, mem_kv, mem_kv, main, None, heads=MEM_HEADS, dq=MEM_HEAD_DIM,
                      dv=MEM_HEAD_DIM, q_blk=q_col // MEM_HEAD_DIM, k_blk=0, v_blk=MEM_HEADS,
                      g_blk=g_col // MEM_HEAD_DIM, mask=None, q_rows=ATTN_Q_ROWS,
                      name="mem_attn")


def _col_scale(widths_and_scales):
    return jnp.concatenate([jnp.full((1, w), s, F32) for w, s in widths_and_scales], axis=1)


def _split_cols(w, sizes):
    out, acc = [], 0
    for s in sizes:
        out.append(w[:, acc:acc + s].astype(BF16))
        acc += s
    return out


def _even_layer(x2d, mem2d, pos_d, freq_row, batch, norm_w, w_in, fox_b_f,
                mla_q_norm_w, mla_w_uq, mla_kv_norm_w, mla_w_ukv, mem_norm_w, w_mem_kv, w_out,
                next_norm_w):
    t = x2d.shape[0]
    seq = t // batch
    (w_a, w_ff, w_b, w_kr, w_c) = _split_cols(
        w_in, (3 * FOX_WIDTH, FOX_HEADS, FOX_WIDTH + MLA_Q_RANK + MLA_KV_RANK,
               MLA_ROPE, MLA_WIDTH + 2 * MEM_WIDTH))
    fq_col, fk_col, fv_col, fg_col = 0, FOX_WIDTH, 2 * FOX_WIDTH, 3 * FOX_WIDTH
    cq_col = 4 * FOX_WIDTH
    ckv_col = cq_col + MLA_Q_RANK
    mg_col = ckv_col + MLA_KV_RANK
    mq_col = mg_col + MLA_WIDTH
    memg_col = mq_col + MEM_WIDTH
    n_main = memg_col + MEM_WIDTH
    ff_lane = MLA_ROPE
    w_tail = jnp.concatenate(
        [w_kr, w_ff, jnp.zeros((D_MODEL, LANES - MLA_ROPE - FOX_HEADS), BF16)], axis=1)
    col_scale = _col_scale([(FOX_WIDTH, FOX_HEAD_DIM ** -0.5 * LOG2E), (mq_col - FOX_WIDTH, 1.0),
                            (MEM_WIDTH, MEM_HEAD_DIM ** -0.5 * LOG2E), (MEM_WIDTH, 1.0)])
    main, tail = _norm_proj(x2d, norm_w, [w_a, w_b, w_c], col_scale, w_tail, rows=PROJ_ROWS)
    main3 = main.reshape(batch, seq, n_main)

    bias_row = jnp.zeros((1, LANES), F32).at[0, ff_lane:ff_lane + FOX_HEADS].set(fox_b_f)
    neg_c = _fox_gate(tail.reshape(batch, seq, LANES), bias_row, lane0=ff_lane, heads=FOX_HEADS)
    o_a = _attention(main3, main3, main3, main3, neg_c.reshape(batch * FOX_HEADS, 1, seq),
                     heads=FOX_HEADS, dq=FOX_HEAD_DIM, dv=FOX_HEAD_DIM,
                     q_blk=fq_col // FOX_HEAD_DIM, k_blk=fk_col // FOX_HEAD_DIM,
                     v_blk=fv_col // FOX_HEAD_DIM, g_blk=fg_col // FOX_HEAD_DIM,
                     mask="causal", q_rows=ATTN_Q_ROWS, name="fox_attn")

    pad = jnp.zeros((MLA_Q_RANK, MLA_HEADS, MLA_QK_PAD - MLA_NOPE - MLA_ROPE), F32)
    w_uq = jnp.concatenate([mla_w_uq.reshape(MLA_Q_RANK, MLA_HEADS, MLA_NOPE + MLA_ROPE), pad],
                           axis=2).reshape(MLA_Q_RANK, MLA_HEADS * MLA_QK_PAD).astype(BF16)
    w_ukv = mla_w_ukv.reshape(MLA_KV_RANK, MLA_HEADS, MLA_NOPE + MLA_V)
    w_ukv = jnp.concatenate([w_ukv[:, :, :MLA_NOPE].reshape(MLA_KV_RANK, -1),
                             w_ukv[:, :, MLA_NOPE:].reshape(MLA_KV_RANK, -1)], axis=1).astype(BF16)
    q_cat, k_cat, v_b = _mla_prep(main, tail, pos_d, freq_row, mla_q_norm_w,
                                  mla_kv_norm_w, w_uq, w_ukv, cq_col=cq_col, ckv_col=ckv_col)
    o_b = _attention(q_cat.reshape(batch, seq, -1), k_cat.reshape(batch, seq, -1),
                     v_b.reshape(batch, seq, -1), main3, None, heads=MLA_HEADS, dq=MLA_QK_PAD,
                     dv=MLA_V, q_blk=0, k_blk=0, v_blk=0, g_blk=mg_col // MLA_V,
                     mask="chunk", q_rows=ATTN_Q_ROWS, name="mla_attn")

    o_m = _memory_branch(main3, mem2d, mem_norm_w, w_mem_kv, batch=batch, q_col=mq_col,
                         g_col=memg_col)
    parts = [o_a.reshape(t, -1), o_b.reshape(t, -1), o_m.reshape(t, -1)]
    return _out_proj(parts, x2d, w_out.astype(BF16), next_norm_w, keep_residual=True)


def _odd_layer(x2d, h2d, mem2d, batch, w_in, gla_w_a2, gla_b_a, gla_norm_w, mem_norm_w,
               w_mem_kv, w_out, final_w):
    t = x2d.shape[0]
    seq = t // batch
    (w_a, w_ga, w_b) = _split_cols(
        w_in, (2 * GLA_KEY_WIDTH + GLA_VAL_WIDTH, GLA_GATE_RANK,
               GLA_VAL_WIDTH + 2 * MEM_WIDTH))
    mq_col = 2 * GLA_KEY_WIDTH + 2 * GLA_VAL_WIDTH
    memg_col = mq_col + MEM_WIDTH
    n_main = memg_col + MEM_WIDTH
    w_tail = jnp.concatenate([w_ga, jnp.zeros((D_MODEL, LANES - GLA_GATE_RANK), BF16)], axis=1)
    col_scale = _col_scale([(GLA_KEY_WIDTH, GLA_DK ** -0.5), (mq_col - GLA_KEY_WIDTH, 1.0),
                            (MEM_WIDTH, MEM_HEAD_DIM ** -0.5 * LOG2E), (MEM_WIDTH, 1.0)])
    main, tail = _norm_proj(h2d, None, [w_a, w_b], col_scale, w_tail, rows=2 * PROJ_ROWS)
    main3 = main.reshape(batch, seq, n_main)

    w_a2 = jnp.concatenate([gla_w_a2, jnp.zeros((LANES - GLA_GATE_RANK, GLA_KEY_WIDTH), F32)],
                           axis=0).astype(BF16)
    o_c = _gla(main3, tail.reshape(batch, seq, LANES), w_a2, gla_b_a, gla_norm_w)
    o_m = _memory_branch(main3, mem2d, mem_norm_w, w_mem_kv, batch=batch, q_col=mq_col,
                         g_col=memg_col)
    parts = [o_c.reshape(t, -1), o_m.reshape(t, -1)]
    return _out_proj(parts, x2d, w_out.astype(BF16), final_w, keep_residual=False)


def kernel(x, mem, positions, e_norm_w, e_w_in, e_fox_b_f, e_mla_q_norm_w, e_mla_w_uq, e_mla_kv_norm_w, e_mla_w_ukv, e_mem_norm_w, e_w_mem_kv, e_w_out, o_norm_w, o_w_in, o_gla_w_a2, o_gla_b_a, o_gla_norm_w, o_mem_norm_w, o_w_mem_kv, o_w_out, final_norm_w):
    batch, seq, d = x.shape
    t = batch * seq
    x2d = x.reshape(t, d)
    mem2d = mem.reshape(-1, d)
    inv_freq = 1.0 / (ROPE_THETA ** (jnp.arange(0, MLA_ROPE, 2, dtype=F32) / MLA_ROPE))
    half = MLA_ROPE // 2
    n_q = LANES // half
    freq_row = jnp.tile(inv_freq, n_q).reshape(1, LANES)
    pos_d = positions.astype(F32).reshape(t // PREP_ROWS, n_q, PREP_ROWS // n_q)
    pos_d = jnp.repeat(pos_d.transpose(0, 2, 1), half, axis=2).reshape(t // n_q, LANES)

    x2d, h2d = _even_layer(x2d, mem2d, pos_d, freq_row, batch, e_norm_w[0], e_w_in[0],
                           e_fox_b_f[0], e_mla_q_norm_w[0], e_mla_w_uq[0], e_mla_kv_norm_w[0],
                           e_mla_w_ukv[0], e_mem_norm_w[0], e_w_mem_kv[0], e_w_out[0],
                           o_norm_w[0])
    out = _odd_layer(x2d, h2d, mem2d, batch, o_w_in[0], o_gla_w_a2[0], o_gla_b_a[0],
                     o_gla_norm_w[0], o_mem_norm_w[0], o_w_mem_kv[0], o_w_out[0], final_norm_w)
    return out.reshape(batch, seq, d)
```

```python
import functools

import jax
import jax.numpy as jnp
from jax import lax
from jax.experimental import pallas as pl
from jax.experimental.pallas import tpu as pltpu

F32 = jnp.float32
BF16 = jnp.bfloat16

D_MODEL = 2048
CHUNK = 64
EPS = 1e-6
NEG = -1e30

FOX_HEADS = 8
FOX_HEAD_DIM = 128
FOX_WIDTH = FOX_HEADS * FOX_HEAD_DIM

MLA_HEADS = 8
MLA_NOPE = 128
MLA_ROPE = 64
MLA_V = 128
MLA_Q_RANK = 512
MLA_KV_RANK = 256
MLA_WIDTH = MLA_HEADS * MLA_V
MLA_QK_PAD = 256
ROPE_THETA = 10000.0

GLA_HEADS = 4
GLA_KEY_WIDTH = D_MODEL // 2
GLA_VAL_WIDTH = D_MODEL
GLA_DK = GLA_KEY_WIDTH // GLA_HEADS
GLA_DV = GLA_VAL_WIDTH // GLA_HEADS
GLA_GATE_RANK = 16
GLA_TAU = 16.0

MEM_HEADS = 4
MEM_HEAD_DIM = 128
MEM_WIDTH = MEM_HEADS * MEM_HEAD_DIM

LANES = 128
VMEM_LIMIT_BYTES = 56 * 1024 * 1024
PROJ_ROWS = 256
PROJ_COLS = 256
OUT_ROWS = 512
OUT_SUB_ROWS = 256
PREP_ROWS = 512
ATTN_Q_ROWS = 256
ATTN_K_COLS = 256
ATTN_MAX_HEADS_PER_STEP = 4
LOG2E = 1.4426950408889634
GLA_BLOCK_ROWS = 256


def _params(*sem):
    return pltpu.CompilerParams(dimension_semantics=sem, vmem_limit_bytes=VMEM_LIMIT_BYTES)


def _rms(x, w):
    return x * lax.rsqrt(jnp.mean(x * x, axis=-1, keepdims=True) + EPS) * w


def _log_sigmoid(z):
    return jnp.minimum(z, 0.0) - jnp.log(1.0 + jnp.exp(-jnp.abs(z)))


def _silu(g):
    h = 0.5 * g
    return h + h * jnp.tanh(h)


def _norm_proj_kernel(x_ref, *rest, n_groups, with_norm, with_tail):
    rest = list(rest)
    nw_ref = rest.pop(0) if with_norm else None
    w_refs = [rest.pop(0) for _ in range(n_groups)]
    cs_ref = rest.pop(0)
    wt_ref = rest.pop(0) if with_tail else None
    o_ref = rest.pop(0)
    ot_ref = rest.pop(0) if with_tail else None
    if with_norm:
        (h_ref,) = rest
        h_ref[...] = _rms(x_ref[...], nw_ref[...]).astype(BF16)
    else:
        h_ref = x_ref
    if with_tail:
        ot_ref[...] = jnp.dot(h_ref[...], wt_ref[...], preferred_element_type=F32)
    c0 = 0
    for w_ref in w_refs:
        for c in range(0, w_ref.shape[1], PROJ_COLS):
            acc = jnp.dot(h_ref[...], w_ref[:, c:c + PROJ_COLS], preferred_element_type=F32)
            dst = slice(c0 + c, c0 + c + PROJ_COLS)
            o_ref[:, dst] = (acc * cs_ref[:, dst]).astype(o_ref.dtype)
        c0 += w_ref.shape[1]


def _norm_proj(x, norm_w, ws, col_scale, w_tail, *, rows):
    t, k = x.shape
    n = sum(w.shape[1] for w in ws)
    with_norm = norm_w is not None
    with_tail = w_tail is not None

    def resident(shape):
        return pl.BlockSpec(shape, lambda i: (0, 0), pipeline_mode=pl.Buffered(1))

    in_specs = [pl.BlockSpec((rows, k), lambda i: (i, 0))]
    args = [x]
    if with_norm:
        in_specs.append(resident((1, k)))
        args.append(norm_w.reshape(1, k))
    in_specs += [resident(w.shape) for w in ws] + [resident((1, n))]
    args += list(ws) + [col_scale]
    out_shape = [jax.ShapeDtypeStruct((t, n), BF16)]
    out_specs = [pl.BlockSpec((rows, n), lambda i: (i, 0))]
    if with_tail:
        in_specs.append(resident((k, LANES)))
        args.append(w_tail)
        out_shape.append(jax.ShapeDtypeStruct((t, LANES), F32))
        out_specs.append(pl.BlockSpec((rows, LANES), lambda i: (i, 0)))
    outs = pl.pallas_call(
        functools.partial(_norm_proj_kernel, n_groups=len(ws), with_norm=with_norm,
                          with_tail=with_tail),
        grid=(t // rows,),
        in_specs=in_specs,
        out_specs=out_specs,
        out_shape=out_shape,
        scratch_shapes=[pltpu.VMEM((rows, k), BF16)] if with_norm else [],
        compiler_params=_params("parallel"),
        name="norm_proj",
    )(*args)
    return outs if with_tail else outs[0]


def _fox_gate_kernel(tail_ref, bias_ref, o_ref, *, lane0, heads):
    seq = tail_ref.shape[1]
    logf = _log_sigmoid(tail_ref[0] + bias_ref[...])
    c = logf.T[lane0:lane0 + heads, :]
    lane = lax.broadcasted_iota(jnp.int32, c.shape, 1)
    shift = 1
    while shift < seq:
        c = c + jnp.where(lane >= shift, pltpu.roll(c, shift, axis=1), 0.0)
        shift *= 2
    o_ref[0] = c * -LOG2E


def _fox_gate(tail, bias_row, *, lane0, heads):
    b, s, _ = tail.shape
    return pl.pallas_call(
        functools.partial(_fox_gate_kernel, lane0=lane0, heads=heads),
        grid=(b,),
        in_specs=[pl.BlockSpec((1, s, LANES), lambda i: (i, 0, 0)),
                  pl.BlockSpec((1, LANES), lambda i: (0, 0))],
        out_specs=pl.BlockSpec((1, heads, s), lambda i: (i, 0, 0)),
        out_shape=jax.ShapeDtypeStruct((b, heads, s), F32),
        compiler_params=_params("parallel"),
        name="fox_gate",
    )(tail, bias_row)


def _attn_kernel(q_ref, k_ref, v_ref, g_ref, *rest, mask, q_rows, with_bias, hps):
    if with_bias:
        b_ref, o_ref, s_ref, v1_ref = rest
    else:
        o_ref, s_ref, v1_ref = rest
    seq, sk = q_ref.shape[1], k_ref.shape[1]
    dq, dv = q_ref.shape[2] // hps, v_ref.shape[2] // hps
    for hh in range(hps):
        v1_ref[hh, :, :dv] = v_ref[0, :, hh * dv:(hh + 1) * dv]
        v1_ref[hh, :, dv:] = jnp.ones((sk, dv), BF16)
    k_cols = ATTN_K_COLS
    nt = (((1,), (1,)), ((), ()))

    def lane_slabs(t):
        return [t[:, c:c + LANES] for c in range(0, t.shape[1], LANES)]

    tasks = [(hh, i) for hh in range(hps) for i in range(seq // q_rows)]

    def tiles(i):
        return range(0, sk if mask is None else (i + 1) * q_rows, k_cols)

    def pass1(n):
        hh, i = tasks[n]
        lo = i * q_rows
        q = q_ref[0, lo:lo + q_rows, hh * dq:(hh + 1) * dq]
        mx = jnp.full((q_rows, LANES), NEG, F32)
        for a in tiles(i):
            s = lax.dot_general(q, k_ref[0, a:a + k_cols, hh * dq:(hh + 1) * dq], nt,
                                preferred_element_type=F32)
            if with_bias:
                s = s + b_ref[hh, :, a:a + k_cols]
            if mask is not None and a + k_cols > lo + 1:
                row = lo + lax.broadcasted_iota(jnp.int32, s.shape, 0)
                col = a + lax.broadcasted_iota(jnp.int32, s.shape, 1)
                if mask == "chunk":
                    shift = CHUNK.bit_length() - 1
                    row, col = row >> shift, col >> shift
                s = jnp.where(row >= col, s, NEG)
            s_ref[n % 2, :, a:a + k_cols] = s
            mx = functools.reduce(jnp.maximum, lane_slabs(s), mx)
        return jnp.max(mx, axis=-1, keepdims=True)

    def pass2(n, m):
        hh, i = tasks[n]
        lo = i * q_rows
        o = jnp.zeros((q_rows, 2 * dv), F32)
        for a in tiles(i):
            p = jnp.exp2(s_ref[n % 2, :, a:a + k_cols] - m)
            o = o + jnp.dot(p.astype(BF16), v1_ref[hh, a:a + k_cols, :],
                            preferred_element_type=F32)
        g = g_ref[0, lo:lo + q_rows, hh * dv:(hh + 1) * dv].astype(F32)
        o_ref[0, lo:lo + q_rows, hh * dv:(hh + 1) * dv] = (
            o[:, :dv] / o[:, dv:] * _silu(g)).astype(o_ref.dtype)

    m = pass1(0)
    for n in range(len(tasks)):
        m_next = pass1(n + 1) if n + 1 < len(tasks) else None
        pass2(n, m)
        m = m_next


def _attention(q, k, v, gate, bias, *, heads, dq, dv, q_blk, k_blk, v_blk, g_blk,
               mask, q_rows, name):
    b, s, _ = q.shape
    sk = k.shape[1]
    hps = next(h for h in (ATTN_MAX_HEADS_PER_STEP, 2, 1)
               if all(n % h == 0 for n in (heads, q_blk, k_blk, v_blk, g_blk)))
    groups = heads // hps
    in_specs = [
        pl.BlockSpec((1, s, hps * dq), lambda i, h: (i, 0, q_blk // hps + h)),
        pl.BlockSpec((1, sk, hps * dq), lambda i, h: (i, 0, k_blk // hps + h)),
        pl.BlockSpec((1, sk, hps * dv), lambda i, h: (i, 0, v_blk // hps + h)),
        pl.BlockSpec((1, s, hps * dv), lambda i, h: (i, 0, g_blk // hps + h)),
    ]
    args = [q, k, v, gate]
    if bias is not None:
        in_specs.append(pl.BlockSpec((hps, 1, sk), lambda i, h: (i * groups + h, 0, 0)))
        args.append(bias)
    return pl.pallas_call(
        functools.partial(_attn_kernel, mask=mask, q_rows=q_rows, with_bias=bias is not None,
                          hps=hps),
        grid=(b, groups),
        in_specs=in_specs,
        out_specs=pl.BlockSpec((1, s, hps * dv), lambda i, h: (i, 0, h)),
        out_shape=jax.ShapeDtypeStruct((b, s, heads * dv), BF16),
        scratch_shapes=[pltpu.VMEM((2, q_rows, sk), F32),
                        pltpu.VMEM((hps, sk, 2 * dv), BF16)],
        compiler_params=_params("parallel", "arbitrary"),
        name=name,
    )(*args)


def _rope(v, cos, sin_signed, lane):
    swapped = jnp.where(lane < MLA_ROPE // 2,
                        pltpu.roll(v, LANES - MLA_ROPE // 2, axis=1),
                        pltpu.roll(v, MLA_ROPE // 2, axis=1))
    return v * cos + swapped * sin_signed


def _mla_prep_kernel(cq_ref, ckv_ref, tail_ref, pos_ref, freq_ref, qnw_ref, kvnw_ref,
                     wuq_ref, wukv_ref, q_ref, k_ref, v_ref):
    rows = cq_ref.shape[0]
    half = MLA_ROPE // 2
    n_q = LANES // half
    sub = rows // n_q
    ang = pos_ref[...] * freq_ref[...]
    cos_d, sin_d = jnp.cos(ang), jnp.sin(ang)
    lane_s = lax.broadcasted_iota(jnp.int32, (sub, LANES), 1)
    cos_parts, sin_parts = [], []
    for j in range(n_q):
        def spread(t, shift0=(LANES - half * j) % LANES, shift1=(LANES - half * j + half) % LANES):
            lo = pltpu.roll(t, shift0, axis=1) if shift0 else t
            hi = pltpu.roll(t, shift1, axis=1) if shift1 else t
            return lo, hi
        c_lo, c_hi = spread(cos_d)
        s_lo, s_hi = spread(sin_d)
        cos_parts.append(jnp.where(lane_s < half, c_lo, c_hi))
        sin_parts.append(jnp.where(lane_s < half, -s_lo, jnp.where(lane_s < 2 * half, s_hi, 0.0)))
    cos = jnp.concatenate(cos_parts, axis=0)
    sin_signed = jnp.concatenate(sin_parts, axis=0)
    lane = lax.broadcasted_iota(jnp.int32, (rows, LANES), 1)
    scale = (MLA_NOPE + MLA_ROPE) ** -0.5 * LOG2E

    cqn = _rms(cq_ref[...].astype(F32), qnw_ref[...]).astype(BF16)
    q = jnp.dot(cqn, wuq_ref[...], preferred_element_type=F32) * scale
    ckvn = _rms(ckv_ref[...].astype(F32), kvnw_ref[...]).astype(BF16)
    kv = jnp.dot(ckvn, wukv_ref[...], preferred_element_type=F32)
    k_rope = jnp.where(lane < MLA_ROPE, _rope(tail_ref[...], cos, sin_signed, lane), 0.0)
    k_rope = k_rope.astype(BF16)
    for h in range(MLA_HEADS):
        c0 = h * MLA_QK_PAD
        q_ref[:, c0:c0 + MLA_NOPE] = q[:, c0:c0 + MLA_NOPE].astype(BF16)
        q_ref[:, c0 + MLA_NOPE:c0 + MLA_QK_PAD] = _rope(
            q[:, c0 + MLA_NOPE:c0 + MLA_QK_PAD], cos, sin_signed, lane).astype(BF16)
        k_ref[:, c0:c0 + MLA_NOPE] = kv[:, h * MLA_NOPE:(h + 1) * MLA_NOPE].astype(BF16)
        k_ref[:, c0 + MLA_NOPE:c0 + MLA_QK_PAD] = k_rope
    v_ref[...] = kv[:, MLA_HEADS * MLA_NOPE:].astype(BF16)


def _mla_prep(main, tail, pos_d, freq_row, q_norm_w, kv_norm_w, w_uq, w_ukv,
              *, cq_col, ckv_col):
    t = main.shape[0]
    rows = PREP_ROWS
    pos_rows = rows * pos_d.shape[0] // t
    const = lambda shape: pl.BlockSpec(shape, lambda i: (0, 0))
    return pl.pallas_call(
        _mla_prep_kernel,
        grid=(t // rows,),
        in_specs=[
            pl.BlockSpec((rows, MLA_Q_RANK), lambda i: (i, cq_col // MLA_Q_RANK)),
            pl.BlockSpec((rows, MLA_KV_RANK), lambda i: (i, ckv_col // MLA_KV_RANK)),
            pl.BlockSpec((rows, LANES), lambda i: (i, 0)),
            pl.BlockSpec((pos_rows, LANES), lambda i: (i, 0)),
            const((1, LANES)),
            const((1, MLA_Q_RANK)), const((1, MLA_KV_RANK)),
            const(w_uq.shape), const(w_ukv.shape),
        ],
        out_specs=[
            pl.BlockSpec((rows, MLA_HEADS * MLA_QK_PAD), lambda i: (i, 0)),
            pl.BlockSpec((rows, MLA_HEADS * MLA_QK_PAD), lambda i: (i, 0)),
            pl.BlockSpec((rows, MLA_WIDTH), lambda i: (i, 0)),
        ],
        out_shape=[
            jax.ShapeDtypeStruct((t, MLA_HEADS * MLA_QK_PAD), BF16),
            jax.ShapeDtypeStruct((t, MLA_HEADS * MLA_QK_PAD), BF16),
            jax.ShapeDtypeStruct((t, MLA_WIDTH), BF16),
        ],
        compiler_params=_params("parallel"),
        name="mla_prep",
    )(main, main, tail, pos_d, freq_row, q_norm_w.reshape(1, -1),
      kv_norm_w.reshape(1, -1), w_uq, w_ukv)


def _gla_kernel(q_ref, k_ref, v_ref, gg_ref, ga_ref, wa_ref, ba_ref, nw_ref, o_ref,
                hl_ref, qd_ref, ki_ref, ke_ref, qx_ref, qb_ref, kb_ref, dec_ref, att_ref,
                oi_ref, kv_ref, sall_ref):
    seq, dk = q_ref.shape[1], q_ref.shape[2]
    rows = GLA_BLOCK_ROWS
    n_c = rows // CHUNK
    n_x = sum(rows - d * CHUNK for d in range(1, n_c))
    shift = CHUNK.bit_length() - 1
    row = lax.broadcasted_iota(jnp.int32, (rows, rows), 0)
    col = lax.broadcasted_iota(jnp.int32, (rows, rows), 1)
    chunk_gap = (row >> shift) - (col >> shift)
    causal = jnp.logical_and(chunk_gap == 0, row >= col)
    tri = causal.astype(BF16)
    ones = (chunk_gap == 0).astype(BF16)
    nt = (((1,), (1,)), ((), ()))
    tn = (((0,), (0,)), ((), ()))
    blocks = list(range(seq // rows))

    def rows_of(b, lo=0, hi=rows):
        return slice(b * rows + lo, b * rows + hi)

    for b in blocks:
        r = rows_of(b)
        z = jnp.dot(ga_ref[0, r, :].astype(BF16), wa_ref[...], preferred_element_type=F32)
        log_a = _log_sigmoid(z + ba_ref[...]) * (LOG2E / GLA_TAU)
        hi = log_a.astype(BF16)
        hl_ref[r, :dk] = hi
        hl_ref[r, dk:] = (log_a - hi.astype(F32)).astype(BF16)

    for b in blocks:
        r = rows_of(b)
        hl = hl_ref[r, :]
        cum2 = jnp.dot(tri, hl, preferred_element_type=F32)
        tot2 = jnp.dot(ones, hl, preferred_element_type=F32)
        zero = jnp.zeros((1, dk), F32)
        t = [tot2[c * CHUNK:c * CHUNK + 1, :dk] + tot2[c * CHUNK:c * CHUNK + 1, dk:]
             for c in range(n_c)]
        x_base = [b * n_x + sum(rows - e * CHUNK for e in range(1, d)) for d in range(n_c)]
        for c in range(n_c):
            rc = slice(c * CHUNK, (c + 1) * CHUNK)
            rg = rows_of(b, c * CHUNK, (c + 1) * CHUNK)
            cum = cum2[rc, :dk] + cum2[rc, dk:]
            k = k_ref[0, rg, :]
            q_dec = q_ref[0, rg, :] * jnp.exp2(cum).astype(BF16)
            k_end = k * jnp.exp2(t[c] - cum).astype(BF16)
            qd_ref[rg, :] = q_dec
            ki_ref[rg, :] = k * jnp.exp2(-cum).astype(BF16)
            ke_ref[rg, :] = k_end
            for d in range(1, c + 1):
                gap = sum(t[c - d + 1:c], zero)
                qx_ref[x_base[d] + (c - d) * CHUNK:x_base[d] + (c - d + 1) * CHUNK, :] = (
                    q_dec * jnp.exp2(gap).astype(BF16))
            qb_ref[rg, :] = q_dec * jnp.exp2(sum(t[:c], zero)).astype(BF16)
            kb_ref[rg, :] = k_end * jnp.exp2(sum(t[c + 1:], zero)).astype(BF16)
        dec_ref[b] = jnp.broadcast_to(jnp.exp2(sum(t[1:], t[0])), dec_ref.shape[1:])

    for b in blocks:
        r = rows_of(b)
        same = lax.dot_general(qd_ref[r, :], ki_ref[r, :], nt, preferred_element_type=F32)
        cross = lax.dot_general(qx_ref[b * n_x:(b + 1) * n_x, :], ke_ref[r, :], nt,
                                preferred_element_type=F32)
        x_base = [sum(rows - e * CHUNK for e in range(1, d)) for d in range(n_c)]
        for c in range(n_c):
            rc = slice(c * CHUNK, (c + 1) * CHUNK)
            att = jnp.where(causal[rc], same[rc], 0.0)
            for d in range(1, c + 1):
                x0 = x_base[d] + (c - d) * CHUNK
                att = jnp.where(chunk_gap[rc] == d, cross[x0:x0 + CHUNK], att)
            att_ref[rows_of(b, c * CHUNK, (c + 1) * CHUNK), :] = att.astype(BF16)

    for b in blocks:
        r = rows_of(b)
        v = v_ref[0, r, :]
        oi_ref[r, :] = jnp.dot(att_ref[r, :], v, preferred_element_type=F32)
        kv_ref[b] = lax.dot_general(v, kb_ref[r, :], tn, preferred_element_type=F32)

    state = jnp.zeros(kv_ref.shape[1:], F32)
    for b in blocks:
        sall_ref[b] = state.astype(BF16)
        if b + 1 < len(blocks):
            state = state * dec_ref[b, 0:1, :] + kv_ref[b]

    def inter(b):
        return lax.dot_general(qb_ref[rows_of(b), :], sall_ref[b], nt,
                               preferred_element_type=F32)

    o_inter = inter(0)
    for b in blocks:
        o_next = inter(b + 1) if b + 1 < len(blocks) else None
        for c in range(n_c):
            rc = rows_of(b, c * CHUNK, (c + 1) * CHUNK)
            o = oi_ref[rc, :] + o_inter[c * CHUNK:(c + 1) * CHUNK]
            o_ref[0, rc, :] = _rms(o, nw_ref[...]).astype(o_ref.dtype) * _silu(gg_ref[0, rc, :])
        o_inter = o_next


def _gla(main, tail, w_a2, b_a, norm_w):
    b, s, _ = main.shape
    n_blocks = s // GLA_BLOCK_ROWS
    n_cross = sum(GLA_BLOCK_ROWS - d * CHUNK for d in range(1, GLA_BLOCK_ROWS // CHUNK))
    return pl.pallas_call(
        _gla_kernel,
        grid=(b, GLA_HEADS),
        in_specs=[
            pl.BlockSpec((1, s, GLA_DK), lambda i, h: (i, 0, h)),
            pl.BlockSpec((1, s, GLA_DK), lambda i, h: (i, 0, GLA_HEADS + h)),
            pl.BlockSpec((1, s, GLA_DV), lambda i, h: (i, 0, GLA_HEADS + h)),
            pl.BlockSpec((1, s, GLA_DV), lambda i, h: (i, 0, 2 * GLA_HEADS + h)),
            pl.BlockSpec((1, s, LANES), lambda i, h: (i, 0, 0)),
            pl.BlockSpec((LANES, GLA_DK), lambda i, h: (0, h)),
            pl.BlockSpec((1, GLA_DK), lambda i, h: (0, h)),
            pl.BlockSpec((1, GLA_DV), lambda i, h: (0, 0)),
        ],
        out_specs=pl.BlockSpec((1, s, GLA_DV), lambda i, h: (i, 0, h)),
        out_shape=jax.ShapeDtypeStruct((b, s, GLA_VAL_WIDTH), BF16),
        scratch_shapes=[pltpu.VMEM((s, 2 * GLA_DK), BF16),
                        pltpu.VMEM((s, GLA_DK), BF16),
                        pltpu.VMEM((s, GLA_DK), BF16),
                        pltpu.VMEM((s, GLA_DK), BF16),
                        pltpu.VMEM((n_blocks * n_cross, GLA_DK), BF16),
                        pltpu.VMEM((s, GLA_DK), BF16),
                        pltpu.VMEM((s, GLA_DK), BF16),
                        pltpu.VMEM((n_blocks, 8, GLA_DK), F32),
                        pltpu.VMEM((s, GLA_BLOCK_ROWS), BF16),
                        pltpu.VMEM((s, GLA_DV), F32),
                        pltpu.VMEM((n_blocks, GLA_DV, GLA_DK), F32),
                        pltpu.VMEM((n_blocks, GLA_DV, GLA_DK), BF16)],
        compiler_params=_params("parallel", "arbitrary"),
        name="gla",
    )(main, main, main, main, tail, w_a2, b_a.reshape(1, -1), norm_w.reshape(1, -1))


def _out_proj_kernel(*refs, n_parts, keep_residual):
    parts = refs[:n_parts]
    x_ref, w_ref, nw_ref = refs[n_parts:n_parts + 3]
    outs = refs[n_parts + 3:]
    for s0 in range(0, x_ref.shape[0], OUT_SUB_ROWS):
        rs = slice(s0, s0 + OUT_SUB_ROWS)
        y = x_ref[rs, :]
        r0 = 0
        for p in parts:
            width = p.shape[1]
            y = y + jnp.dot(p[rs, :], w_ref[r0:r0 + width, :], preferred_element_type=F32)
            r0 += width
        if keep_residual:
            y_ref, h_ref = outs
            y_ref[rs, :] = y
        else:
            (h_ref,) = outs
        h_ref[rs, :] = _rms(y, nw_ref[...]).astype(h_ref.dtype)


def _out_proj(parts, x, w, norm_w, *, keep_residual):
    t, d = x.shape
    rows = OUT_ROWS
    in_specs = [pl.BlockSpec((rows, p.shape[1]), lambda i: (i, 0)) for p in parts]
    in_specs += [pl.BlockSpec((rows, d), lambda i: (i, 0)),
                 pl.BlockSpec(w.shape, lambda i: (0, 0), pipeline_mode=pl.Buffered(1)),
                 pl.BlockSpec((1, d), lambda i: (0, 0))]
    args = list(parts) + [x, w, norm_w.reshape(1, d)]
    row_spec = pl.BlockSpec((rows, d), lambda i: (i, 0))
    if keep_residual:
        out_specs = [row_spec, row_spec]
        out_shape = [jax.ShapeDtypeStruct((t, d), F32), jax.ShapeDtypeStruct((t, d), BF16)]
    else:
        out_specs = row_spec
        out_shape = jax.ShapeDtypeStruct((t, d), F32)
    return pl.pallas_call(
        functools.partial(_out_proj_kernel, n_parts=len(parts), keep_residual=keep_residual),
        grid=(t // rows,),
        in_specs=in_specs,
        out_specs=out_specs,
        out_shape=out_shape,
        compiler_params=_params("parallel"),
        name="out_proj",
    )(*args)


def _memory_branch(main, mem2d, mem_norm_w, w_mem_kv, *, batch, q_col, g_col):
    n = w_mem_kv.shape[1]
    mem_kv = _norm_proj(mem2d, mem_norm_w, [w_mem_kv.astype(BF16)], jnp.ones((1, n), F32), None,
                        rows=PROJ_ROWS)
    mem_kv = mem_kv.reshape(batch, -1, n)
    return _attention(main, mem_kv, mem_kv, main, None, heads=MEM_HEADS, dq=MEM_HEAD_DIM,
                      dv=MEM_HEAD_DIM, q_blk=q_col // MEM_HEAD_DIM, k_blk=0, v_blk=MEM_HEADS,
                      g_blk=g_col // MEM_HEAD_DIM, mask=None, q_rows=ATTN_Q_ROWS,
                      name="mem_attn")


def _col_scale(widths_and_scales):
    return jnp.concatenate([jnp.full((1, w), s, F32) for w, s in widths_and_scales], axis=1)


def _split_cols(w, sizes):
    out, acc = [], 0
    for s in sizes:
        out.append(w[:, acc:acc + s].astype(BF16))
        acc += s
    return out


def _even_layer(x2d, mem2d, pos_d, freq_row, batch, norm_w, w_in, fox_b_f,
                mla_q_norm_w, mla_w_uq, mla_kv_norm_w, mla_w_ukv, mem_norm_w, w_mem_kv, w_out,
                next_norm_w):
    t = x2d.shape[0]
    seq = t // batch
    (w_a, w_ff, w_b, w_kr, w_c) = _split_cols(
        w_in, (3 * FOX_WIDTH, FOX_HEADS, FOX_WIDTH + MLA_Q_RANK + MLA_KV_RANK,
               MLA_ROPE, MLA_WIDTH + 2 * MEM_WIDTH))
    fq_col, fk_col, fv_col = 0, FOX_WIDTH, 2 * FOX_WIDTH
    mg_col = 3 * FOX_WIDTH
    mq_col = mg_col + MLA_WIDTH
    memg_col = mq_col + MEM_WIDTH
    fg_col = memg_col + MEM_WIDTH
    cq_col = fg_col + FOX_WIDTH
    ckv_col = cq_col + MLA_Q_RANK
    n_main = ckv_col + MLA_KV_RANK
    ff_lane = MLA_ROPE
    w_tail = jnp.concatenate(
        [w_kr, w_ff, jnp.zeros((D_MODEL, LANES - MLA_ROPE - FOX_HEADS), BF16)], axis=1)
    col_scale = _col_scale([(FOX_WIDTH, FOX_HEAD_DIM ** -0.5 * LOG2E), (mq_col - FOX_WIDTH, 1.0),
                            (MEM_WIDTH, MEM_HEAD_DIM ** -0.5 * LOG2E),
                            (n_main - memg_col, 1.0)])
    main, tail = _norm_proj(x2d, norm_w, [w_a, w_c, w_b], col_scale, w_tail, rows=PROJ_ROWS)
    main3 = main.reshape(batch, seq, n_main)

    bias_row = jnp.zeros((1, LANES), F32).at[0, ff_lane:ff_lane + FOX_HEADS].set(fox_b_f)
    neg_c = _fox_gate(tail.reshape(batch, seq, LANES), bias_row, lane0=ff_lane, heads=FOX_HEADS)
    o_a = _attention(main3, main3, main3, main3, neg_c.reshape(batch * FOX_HEADS, 1, seq),
                     heads=FOX_HEADS, dq=FOX_HEAD_DIM, dv=FOX_HEAD_DIM,
                     q_blk=fq_col // FOX_HEAD_DIM, k_blk=fk_col // FOX_HEAD_DIM,
                     v_blk=fv_col // FOX_HEAD_DIM, g_blk=fg_col // FOX_HEAD_DIM,
                     mask="causal", q_rows=ATTN_Q_ROWS, name="fox_attn")

    pad = jnp.zeros((MLA_Q_RANK, MLA_HEADS, MLA_QK_PAD - MLA_NOPE - MLA_ROPE), F32)
    w_uq = jnp.concatenate([mla_w_uq.reshape(MLA_Q_RANK, MLA_HEADS, MLA_NOPE + MLA_ROPE), pad],
                           axis=2).reshape(MLA_Q_RANK, MLA_HEADS * MLA_QK_PAD).astype(BF16)
    w_ukv = mla_w_ukv.reshape(MLA_KV_RANK, MLA_HEADS, MLA_NOPE + MLA_V)
    w_ukv = jnp.concatenate([w_ukv[:, :, :MLA_NOPE].reshape(MLA_KV_RANK, -1),
                             w_ukv[:, :, MLA_NOPE:].reshape(MLA_KV_RANK, -1)], axis=1).astype(BF16)
    q_cat, k_cat, v_b = _mla_prep(main, tail, pos_d, freq_row, mla_q_norm_w,
                                  mla_kv_norm_w, w_uq, w_ukv, cq_col=cq_col, ckv_col=ckv_col)
    o_b = _attention(q_cat.reshape(batch, seq, -1), k_cat.reshape(batch, seq, -1),
                     v_b.reshape(batch, seq, -1), main3, None, heads=MLA_HEADS, dq=MLA_QK_PAD,
                     dv=MLA_V, q_blk=0, k_blk=0, v_blk=0, g_blk=mg_col // MLA_V,
                     mask="chunk", q_rows=ATTN_Q_ROWS, name="mla_attn")

    o_m = _memory_branch(main3, mem2d, mem_norm_w, w_mem_kv, batch=batch, q_col=mq_col,
                         g_col=memg_col)
    parts = [o_a.reshape(t, -1), o_b.reshape(t, -1), o_m.reshape(t, -1)]
    return _out_proj(parts, x2d, w_out.astype(BF16), next_norm_w, keep_residual=True)


def _odd_layer(x2d, h2d, mem2d, batch, w_in, gla_w_a2, gla_b_a, gla_norm_w, mem_norm_w,
               w_mem_kv, w_out, final_w):
    t = x2d.shape[0]
    seq = t // batch
    (w_a, w_ga, w_b) = _split_cols(
        w_in, (2 * GLA_KEY_WIDTH + GLA_VAL_WIDTH, GLA_GATE_RANK,
               GLA_VAL_WIDTH + 2 * MEM_WIDTH))
    mq_col = 2 * GLA_KEY_WIDTH + 2 * GLA_VAL_WIDTH
    memg_col = mq_col + MEM_WIDTH
    n_main = memg_col + MEM_WIDTH
    w_tail = jnp.concatenate([w_ga, jnp.zeros((D_MODEL, LANES - GLA_GATE_RANK), BF16)], axis=1)
    col_scale = _col_scale([(GLA_KEY_WIDTH, GLA_DK ** -0.5), (mq_col - GLA_KEY_WIDTH, 1.0),
                            (MEM_WIDTH, MEM_HEAD_DIM ** -0.5 * LOG2E), (MEM_WIDTH, 1.0)])
    main, tail = _norm_proj(h2d, None, [w_a, w_b], col_scale, w_tail, rows=2 * PROJ_ROWS)
    main3 = main.reshape(batch, seq, n_main)

    w_a2 = jnp.concatenate([gla_w_a2, jnp.zeros((LANES - GLA_GATE_RANK, GLA_KEY_WIDTH), F32)],
                           axis=0).astype(BF16)
    o_c = _gla(main3, tail.reshape(batch, seq, LANES), w_a2, gla_b_a, gla_norm_w)
    o_m = _memory_branch(main3, mem2d, mem_norm_w, w_mem_kv, batch=batch, q_col=mq_col,
                         g_col=memg_col)
    parts = [o_c.reshape(t, -1), o_m.reshape(t, -1)]
    return _out_proj(parts, x2d, w_out.astype(BF16), final_w, keep_residual=False)


def kernel(x, mem, positions, e_norm_w, e_w_in, e_fox_b_f, e_mla_q_norm_w, e_mla_w_uq, e_mla_kv_norm_w, e_mla_w_ukv, e_mem_norm_w, e_w_mem_kv, e_w_out, o_norm_w, o_w_in, o_gla_w_a2, o_gla_b_a, o_gla_norm_w, o_mem_norm_w, o_w_mem_kv, o_w_out, final_norm_w):
    batch, seq, d = x.shape
    t = batch * seq
    x2d = x.reshape(t, d)
    mem2d = mem.reshape(-1, d)
    inv_freq = 1.0 / (ROPE_THETA ** (jnp.arange(0, MLA_ROPE, 2, dtype=F32) / MLA_ROPE))
    half = MLA_ROPE // 2
    n_q = LANES // half
    freq_row = jnp.tile(inv_freq, n_q).reshape(1, LANES)
    pos_d = positions.astype(F32).reshape(t // PREP_ROWS, n_q, PREP_ROWS // n_q)
    pos_d = jnp.repeat(pos_d.transpose(0, 2, 1), half, axis=2).reshape(t // n_q, LANES)

    x2d, h2d = _even_layer(x2d, mem2d, pos_d, freq_row, batch, e_norm_w[0], e_w_in[0],
                           e_fox_b_f[0], e_mla_q_norm_w[0], e_mla_w_uq[0], e_mla_kv_norm_w[0],
                           e_mla_w_ukv[0], e_mem_norm_w[0], e_w_mem_kv[0], e_w_out[0],
                           o_norm_w[0])
    out = _odd_layer(x2d, h2d, mem2d, batch, o_w_in[0], o_gla_w_a2[0], o_gla_b_a[0],
                     o_gla_norm_w[0], o_mem_norm_w[0], o_w_mem_kv[0], o_w_out[0], final_norm_w)
    return out.reshape(batch, seq, d)
```

```python
import functools

import jax
import jax.numpy as jnp
from jax import lax
from jax.experimental import pallas as pl
from jax.experimental.pallas import tpu as pltpu

F32 = jnp.float32
BF16 = jnp.bfloat16

D_MODEL = 2048
CHUNK = 64
EPS = 1e-6
NEG = -1e30

FOX_HEADS = 8
FOX_HEAD_DIM = 128
FOX_WIDTH = FOX_HEADS * FOX_HEAD_DIM

MLA_HEADS = 8
MLA_NOPE = 128
MLA_ROPE = 64
MLA_V = 128
MLA_Q_RANK = 512
MLA_KV_RANK = 256
MLA_WIDTH = MLA_HEADS * MLA_V
MLA_QK_PAD = 256
ROPE_THETA = 10000.0

GLA_HEADS = 4
GLA_KEY_WIDTH = D_MODEL // 2
GLA_VAL_WIDTH = D_MODEL
GLA_DK = GLA_KEY_WIDTH // GLA_HEADS
GLA_DV = GLA_VAL_WIDTH // GLA_HEADS
GLA_GATE_RANK = 16
GLA_TAU = 16.0

MEM_HEADS = 4
MEM_HEAD_DIM = 128
MEM_WIDTH = MEM_HEADS * MEM_HEAD_DIM

LANES = 128
VMEM_LIMIT_BYTES = 56 * 1024 * 1024
PROJ_ROWS = 256
PROJ_COLS = 256
OUT_ROWS = 512
OUT_SUB_ROWS = 256
PREP_ROWS = 512
ATTN_Q_ROWS = 256
ATTN_K_COLS = 256
ATTN_MAX_HEADS_PER_STEP = 4
LOG2E = 1.4426950408889634
GLA_BLOCK_ROWS = 256


def _params(*sem):
    return pltpu.CompilerParams(dimension_semantics=sem, vmem_limit_bytes=VMEM_LIMIT_BYTES)


def _rms(x, w):
    return x * lax.rsqrt(jnp.mean(x * x, axis=-1, keepdims=True) + EPS) * w


def _log_sigmoid(z):
    return jnp.minimum(z, 0.0) - jnp.log(1.0 + jnp.exp(-jnp.abs(z)))


def _silu(g):
    h = 0.5 * g
    return h + h * jnp.tanh(h)


def _norm_proj_kernel(x_ref, *rest, n_groups, with_norm, with_tail):
    rest = list(rest)
    nw_ref = rest.pop(0) if with_norm else None
    w_refs = [rest.pop(0) for _ in range(n_groups)]
    cs_ref = rest.pop(0)
    wt_ref = rest.pop(0) if with_tail else None
    o_ref = rest.pop(0)
    ot_ref = rest.pop(0) if with_tail else None
    if with_norm:
        (h_ref,) = rest
        h_ref[...] = _rms(x_ref[...], nw_ref[...]).astype(BF16)
    else:
        h_ref = x_ref
    if with_tail:
        ot_ref[...] = jnp.dot(h_ref[...], wt_ref[...], preferred_element_type=F32)
    c0 = 0
    for w_ref in w_refs:
        for c in range(0, w_ref.shape[1], PROJ_COLS):
            acc = jnp.dot(h_ref[...], w_ref[:, c:c + PROJ_COLS], preferred_element_type=F32)
            dst = slice(c0 + c, c0 + c + PROJ_COLS)
            o_ref[:, dst] = (acc * cs_ref[:, dst]).astype(o_ref.dtype)
        c0 += w_ref.shape[1]


def _norm_proj(x, norm_w, ws, col_scale, w_tail, *, rows):
    t, k = x.shape
    n = sum(w.shape[1] for w in ws)
    with_norm = norm_w is not None
    with_tail = w_tail is not None

    def resident(shape):
        return pl.BlockSpec(shape, lambda i: (0, 0), pipeline_mode=pl.Buffered(1))

    in_specs = [pl.BlockSpec((rows, k), lambda i: (i, 0))]
    args = [x]
    if with_norm:
        in_specs.append(resident((1, k)))
        args.append(norm_w.reshape(1, k))
    in_specs += [resident(w.shape) for w in ws] + [resident((1, n))]
    args += list(ws) + [col_scale]
    out_shape = [jax.ShapeDtypeStruct((t, n), BF16)]
    out_specs = [pl.BlockSpec((rows, n), lambda i: (i, 0))]
    if with_tail:
        in_specs.append(resident((k, LANES)))
        args.append(w_tail)
        out_shape.append(jax.ShapeDtypeStruct((t, LANES), F32))
        out_specs.append(pl.BlockSpec((rows, LANES), lambda i: (i, 0)))
    outs = pl.pallas_call(
        functools.partial(_norm_proj_kernel, n_groups=len(ws), with_norm=with_norm,
                          with_tail=with_tail),
        grid=(t // rows,),
        in_specs=in_specs,
        out_specs=out_specs,
        out_shape=out_shape,
        scratch_shapes=[pltpu.VMEM((rows, k), BF16)] if with_norm else [],
        compiler_params=_params("parallel"),
        name="norm_proj",
    )(*args)
    return outs if with_tail else outs[0]


def _fox_gate_kernel(tail_ref, bias_ref, o_ref, *, lane0, heads):
    seq = tail_ref.shape[1]
    logf = _log_sigmoid(tail_ref[0] + bias_ref[...])
    c = logf.T[lane0:lane0 + heads, :]
    lane = lax.broadcasted_iota(jnp.int32, c.shape, 1)
    shift = 1
    while shift < seq:
        c = c + jnp.where(lane >= shift, pltpu.roll(c, shift, axis=1), 0.0)
        shift *= 2
    o_ref[0] = c * -LOG2E


def _fox_gate(tail, bias_row, *, lane0, heads):
    b, s, _ = tail.shape
    return pl.pallas_call(
        functools.partial(_fox_gate_kernel, lane0=lane0, heads=heads),
        grid=(b,),
        in_specs=[pl.BlockSpec((1, s, LANES), lambda i: (i, 0, 0)),
                  pl.BlockSpec((1, LANES), lambda i: (0, 0))],
        out_specs=pl.BlockSpec((1, heads, s), lambda i: (i, 0, 0)),
        out_shape=jax.ShapeDtypeStruct((b, heads, s), F32),
        compiler_params=_params("parallel"),
        name="fox_gate",
    )(tail, bias_row)


def _attn_kernel(q_ref, k_ref, v_ref, g_ref, *rest, mask, q_rows, with_bias, hps):
    if with_bias:
        b_ref, o_ref, s_ref, v1_ref = rest
    else:
        o_ref, s_ref, v1_ref = rest
    seq, sk = q_ref.shape[1], k_ref.shape[1]
    dq, dv = q_ref.shape[2] // hps, v_ref.shape[2] // hps
    for hh in range(hps):
        v1_ref[hh, :, :dv] = v_ref[0, :, hh * dv:(hh + 1) * dv]
        v1_ref[hh, :, dv:] = jnp.ones((sk, dv), BF16)
    k_cols = ATTN_K_COLS
    nt = (((1,), (1,)), ((), ()))

    def lane_slabs(t):
        return [t[:, c:c + LANES] for c in range(0, t.shape[1], LANES)]

    tasks = [(hh, i) for hh in range(hps) for i in range(seq // q_rows)]

    def tiles(i):
        return range(0, sk if mask is None else (i + 1) * q_rows, k_cols)

    def pass1(n):
        hh, i = tasks[n]
        lo = i * q_rows
        q = q_ref[0, lo:lo + q_rows, hh * dq:(hh + 1) * dq]
        mx = jnp.full((q_rows, LANES), NEG, F32)
        for a in tiles(i):
            s = lax.dot_general(q, k_ref[0, a:a + k_cols, hh * dq:(hh + 1) * dq], nt,
                                preferred_element_type=F32)
            if with_bias:
                s = s + b_ref[hh, :, a:a + k_cols]
            if mask is not None and a + k_cols > lo + 1:
                row = lo + lax.broadcasted_iota(jnp.int32, s.shape, 0)
                col = a + lax.broadcasted_iota(jnp.int32, s.shape, 1)
                if mask == "chunk":
                    shift = CHUNK.bit_length() - 1
                    row, col = row >> shift, col >> shift
                s = jnp.where(row >= col, s, NEG)
            s_ref[n % 2, :, a:a + k_cols] = s
            mx = functools.reduce(jnp.maximum, lane_slabs(s), mx)
        return jnp.max(mx, axis=-1, keepdims=True)

    def pass2(n, m):
        hh, i = tasks[n]
        lo = i * q_rows
        o = jnp.zeros((q_rows, 2 * dv), F32)
        for a in tiles(i):
            p = jnp.exp2(s_ref[n % 2, :, a:a + k_cols] - m)
            o = o + jnp.dot(p.astype(BF16), v1_ref[hh, a:a + k_cols, :],
                            preferred_element_type=F32)
        g = g_ref[0, lo:lo + q_rows, hh * dv:(hh + 1) * dv].astype(F32)
        o_ref[0, lo:lo + q_rows, hh * dv:(hh + 1) * dv] = (
            o[:, :dv] / o[:, dv:] * _silu(g)).astype(o_ref.dtype)

    m = pass1(0)
    for n in range(len(tasks)):
        m_next = pass1(n + 1) if n + 1 < len(tasks) else None
        pass2(n, m)
        m = m_next


def _attention(q, k, v, gate, bias, *, heads, dq, dv, q_blk, k_blk, v_blk, g_blk,
               mask, q_rows, name):
    b, s, _ = q.shape
    sk = k.shape[1]
    hps = next(h for h in (ATTN_MAX_HEADS_PER_STEP, 2, 1)
               if all(n % h == 0 for n in (heads, q_blk, k_blk, v_blk, g_blk)))
    groups = heads // hps
    in_specs = [
        pl.BlockSpec((1, s, hps * dq), lambda i, h: (i, 0, q_blk // hps + h)),
        pl.BlockSpec((1, sk, hps * dq), lambda i, h: (i, 0, k_blk // hps + h)),
        pl.BlockSpec((1, sk, hps * dv), lambda i, h: (i, 0, v_blk // hps + h)),
        pl.BlockSpec((1, s, hps * dv), lambda i, h: (i, 0, g_blk // hps + h)),
    ]
    args = [q, k, v, gate]
    if bias is not None:
        in_specs.append(pl.BlockSpec((hps, 1, sk), lambda i, h: (i * groups + h, 0, 0)))
        args.append(bias)
    return pl.pallas_call(
        functools.partial(_attn_kernel, mask=mask, q_rows=q_rows, with_bias=bias is not None,
                          hps=hps),
        grid=(b, groups),
        in_specs=in_specs,
        out_specs=pl.BlockSpec((1, s, hps * dv), lambda i, h: (i, 0, h)),
        out_shape=jax.ShapeDtypeStruct((b, s, heads * dv), BF16),
        scratch_shapes=[pltpu.VMEM((2, q_rows, sk), F32),
                        pltpu.VMEM((hps, sk, 2 * dv), BF16)],
        compiler_params=_params("parallel", "arbitrary"),
        name=name,
    )(*args)


def _rope(v, cos, sin_signed, lane):
    swapped = jnp.where(lane < MLA_ROPE // 2,
                        pltpu.roll(v, LANES - MLA_ROPE // 2, axis=1),
                        pltpu.roll(v, MLA_ROPE // 2, axis=1))
    return v * cos + swapped * sin_signed


def _mla_prep_kernel(cq_ref, ckv_ref, tail_ref, pos_ref, freq_ref, qnw_ref, kvnw_ref,
                     wuq_ref, wukv_ref, q_ref, k_ref, v_ref):
    rows = cq_ref.shape[0]
    half = MLA_ROPE // 2
    n_q = LANES // half
    sub = rows // n_q
    ang = pos_ref[...] * freq_ref[...]
    cos_d, sin_d = jnp.cos(ang), jnp.sin(ang)
    lane_s = lax.broadcasted_iota(jnp.int32, (sub, LANES), 1)
    cos_parts, sin_parts = [], []
    for j in range(n_q):
        def spread(t, shift0=(LANES - half * j) % LANES, shift1=(LANES - half * j + half) % LANES):
            lo = pltpu.roll(t, shift0, axis=1) if shift0 else t
            hi = pltpu.roll(t, shift1, axis=1) if shift1 else t
            return lo, hi
        c_lo, c_hi = spread(cos_d)
        s_lo, s_hi = spread(sin_d)
        cos_parts.append(jnp.where(lane_s < half, c_lo, c_hi))
        sin_parts.append(jnp.where(lane_s < half, -s_lo, jnp.where(lane_s < 2 * half, s_hi, 0.0)))
    cos = jnp.concatenate(cos_parts, axis=0)
    sin_signed = jnp.concatenate(sin_parts, axis=0)
    lane = lax.broadcasted_iota(jnp.int32, (rows, LANES), 1)
    scale = (MLA_NOPE + MLA_ROPE) ** -0.5 * LOG2E

    cqn = _rms(cq_ref[...].astype(F32), qnw_ref[...]).astype(BF16)
    q = jnp.dot(cqn, wuq_ref[...], preferred_element_type=F32) * scale
    ckvn = _rms(ckv_ref[...].astype(F32), kvnw_ref[...]).astype(BF16)
    kv = jnp.dot(ckvn, wukv_ref[...], preferred_element_type=F32)
    k_rope = jnp.where(lane < MLA_ROPE, _rope(tail_ref[...], cos, sin_signed, lane), 0.0)
    k_rope = k_rope.astype(BF16)
    for h in range(MLA_HEADS):
        c0 = h * MLA_QK_PAD
        q_ref[:, c0:c0 + MLA_NOPE] = q[:, c0:c0 + MLA_NOPE].astype(BF16)
        q_ref[:, c0 + MLA_NOPE:c0 + MLA_QK_PAD] = _rope(
            q[:, c0 + MLA_NOPE:c0 + MLA_QK_PAD], cos, sin_signed, lane).astype(BF16)
        k_ref[:, c0:c0 + MLA_NOPE] = kv[:, h * MLA_NOPE:(h + 1) * MLA_NOPE].astype(BF16)
        k_ref[:, c0 + MLA_NOPE:c0 + MLA_QK_PAD] = k_rope
    v_ref[...] = kv[:, MLA_HEADS * MLA_NOPE:].astype(BF16)


def _mla_prep(main, tail, pos_d, freq_row, q_norm_w, kv_norm_w, w_uq, w_ukv,
              *, cq_col, ckv_col):
    t = main.shape[0]
    rows = PREP_ROWS
    pos_rows = rows * pos_d.shape[0] // t
    const = lambda shape: pl.BlockSpec(shape, lambda i: (0, 0))
    return pl.pallas_call(
        _mla_prep_kernel,
        grid=(t // rows,),
        in_specs=[
            pl.BlockSpec((rows, MLA_Q_RANK), lambda i: (i, cq_col // MLA_Q_RANK)),
            pl.BlockSpec((rows, MLA_KV_RANK), lambda i: (i, ckv_col // MLA_KV_RANK)),
            pl.BlockSpec((rows, LANES), lambda i: (i, 0)),
            pl.BlockSpec((pos_rows, LANES), lambda i: (i, 0)),
            const((1, LANES)),
            const((1, MLA_Q_RANK)), const((1, MLA_KV_RANK)),
            const(w_uq.shape), const(w_ukv.shape),
        ],
        out_specs=[
            pl.BlockSpec((rows, MLA_HEADS * MLA_QK_PAD), lambda i: (i, 0)),
            pl.BlockSpec((rows, MLA_HEADS * MLA_QK_PAD), lambda i: (i, 0)),
            pl.BlockSpec((rows, MLA_WIDTH), lambda i: (i, 0)),
        ],
        out_shape=[
            jax.ShapeDtypeStruct((t, MLA_HEADS * MLA_QK_PAD), BF16),
            jax.ShapeDtypeStruct((t, MLA_HEADS * MLA_QK_PAD), BF16),
            jax.ShapeDtypeStruct((t, MLA_WIDTH), BF16),
        ],
        compiler_params=_params("parallel"),
        name="mla_prep",
    )(main, main, tail, pos_d, freq_row, q_norm_w.reshape(1, -1),
      kv_norm_w.reshape(1, -1), w_uq, w_ukv)


def _gla_kernel(q_ref, k_ref, v_ref, gg_ref, ga_ref, wa_ref, ba_ref, nw_ref, o_ref,
                hl_ref, qd_ref, ki_ref, ke_ref, qx_ref, qb_ref, kb_ref, dec_ref, att_ref,
                oi_ref, kv_ref, sall_ref):
    seq, dk = q_ref.shape[1], q_ref.shape[2]
    rows = GLA_BLOCK_ROWS
    n_c = rows // CHUNK
    n_x = sum(rows - d * CHUNK for d in range(1, n_c))
    shift = CHUNK.bit_length() - 1
    row = lax.broadcasted_iota(jnp.int32, (rows, rows), 0)
    col = lax.broadcasted_iota(jnp.int32, (rows, rows), 1)
    chunk_gap = (row >> shift) - (col >> shift)
    causal = jnp.logical_and(chunk_gap == 0, row >= col)
    tri = causal.astype(BF16)
    nt = (((1,), (1,)), ((), ()))
    tn = (((0,), (0,)), ((), ()))
    blocks = list(range(seq // rows))

    def rows_of(b, lo=0, hi=rows):
        return slice(b * rows + lo, b * rows + hi)

    for b in blocks:
        r = rows_of(b)
        z = jnp.dot(ga_ref[0, r, :].astype(BF16), wa_ref[...], preferred_element_type=F32)
        log_a = _log_sigmoid(z + ba_ref[...]) * (LOG2E / GLA_TAU)
        hi = log_a.astype(BF16)
        hl_ref[r, :dk] = hi
        hl_ref[r, dk:] = (log_a - hi.astype(F32)).astype(BF16)

    for b in blocks:
        r = rows_of(b)
        hl = hl_ref[r, :]
        cum2 = jnp.dot(tri, hl, preferred_element_type=F32)
        zero = jnp.zeros((1, dk), F32)
        last = [(c + 1) * CHUNK - 1 for c in range(n_c)]
        t = [cum2[r1:r1 + 1, :dk] + cum2[r1:r1 + 1, dk:] for r1 in last]
        x_base = [b * n_x + sum(rows - e * CHUNK for e in range(1, d)) for d in range(n_c)]
        for c in range(n_c):
            rc = slice(c * CHUNK, (c + 1) * CHUNK)
            rg = rows_of(b, c * CHUNK, (c + 1) * CHUNK)
            cum = cum2[rc, :dk] + cum2[rc, dk:]
            k = k_ref[0, rg, :]
            q_dec = q_ref[0, rg, :] * jnp.exp2(cum).astype(BF16)
            k_end = k * jnp.exp2(t[c] - cum).astype(BF16)
            qd_ref[rg, :] = q_dec
            ki_ref[rg, :] = k * jnp.exp2(-cum).astype(BF16)
            ke_ref[rg, :] = k_end
            for d in range(1, c + 1):
                gap = sum(t[c - d + 1:c], zero)
                qx_ref[x_base[d] + (c - d) * CHUNK:x_base[d] + (c - d + 1) * CHUNK, :] = (
                    q_dec * jnp.exp2(gap).astype(BF16))
            qb_ref[rg, :] = q_dec * jnp.exp2(sum(t[:c], zero)).astype(BF16)
            kb_ref[rg, :] = k_end * jnp.exp2(sum(t[c + 1:], zero)).astype(BF16)
        dec_ref[b] = jnp.broadcast_to(jnp.exp2(sum(t[1:], t[0])), dec_ref.shape[1:])

    for b in blocks:
        r = rows_of(b)
        same = lax.dot_general(qd_ref[r, :], ki_ref[r, :], nt, preferred_element_type=F32)
        cross = lax.dot_general(qx_ref[b * n_x:(b + 1) * n_x, :], ke_ref[r, :], nt,
                                preferred_element_type=F32)
        x_base = [sum(rows - e * CHUNK for e in range(1, d)) for d in range(n_c)]
        for c in range(n_c):
            rc = slice(c * CHUNK, (c + 1) * CHUNK)
            att = jnp.where(causal[rc], same[rc], 0.0)
            for d in range(1, c + 1):
                x0 = x_base[d] + (c - d) * CHUNK
                att = jnp.where(chunk_gap[rc] == d, cross[x0:x0 + CHUNK], att)
            att_ref[rows_of(b, c * CHUNK, (c + 1) * CHUNK), :] = att.astype(BF16)

    for b in blocks:
        r = rows_of(b)
        v = v_ref[0, r, :]
        oi_ref[r, :] = jnp.dot(att_ref[r, :], v, preferred_element_type=F32)
        kv_ref[b] = lax.dot_general(v, kb_ref[r, :], tn, preferred_element_type=F32)

    state = jnp.zeros(kv_ref.shape[1:], F32)
    for b in blocks:
        sall_ref[b] = state.astype(BF16)
        if b + 1 < len(blocks):
            state = state * dec_ref[b, 0:1, :] + kv_ref[b]

    def inter(b):
        return lax.dot_general(qb_ref[rows_of(b), :], sall_ref[b], nt,
                               preferred_element_type=F32)

    o_inter = inter(0)
    for b in blocks:
        o_next = inter(b + 1) if b + 1 < len(blocks) else None
        for c in range(n_c):
            rc = rows_of(b, c * CHUNK, (c + 1) * CHUNK)
            o = oi_ref[rc, :] + o_inter[c * CHUNK:(c + 1) * CHUNK]
            o_ref[0, rc, :] = _rms(o, nw_ref[...]).astype(o_ref.dtype) * _silu(gg_ref[0, rc, :])
        o_inter = o_next


def _gla(main, tail, w_a2, b_a, norm_w):
    b, s, _ = main.shape
    n_blocks = s // GLA_BLOCK_ROWS
    n_cross = sum(GLA_BLOCK_ROWS - d * CHUNK for d in range(1, GLA_BLOCK_ROWS // CHUNK))
    return pl.pallas_call(
        _gla_kernel,
        grid=(b, GLA_HEADS),
        in_specs=[
            pl.BlockSpec((1, s, GLA_DK), lambda i, h: (i, 0, h)),
            pl.BlockSpec((1, s, GLA_DK), lambda i, h: (i, 0, GLA_HEADS + h)),
            pl.BlockSpec((1, s, GLA_DV), lambda i, h: (i, 0, GLA_HEADS + h)),
            pl.BlockSpec((1, s, GLA_DV), lambda i, h: (i, 0, 2 * GLA_HEADS + h)),
            pl.BlockSpec((1, s, LANES), lambda i, h: (i, 0, 0)),
            pl.BlockSpec((LANES, GLA_DK), lambda i, h: (0, h)),
            pl.BlockSpec((1, GLA_DK), lambda i, h: (0, h)),
            pl.BlockSpec((1, GLA_DV), lambda i, h: (0, 0)),
        ],
        out_specs=pl.BlockSpec((1, s, GLA_DV), lambda i, h: (i, 0, h)),
        out_shape=jax.ShapeDtypeStruct((b, s, GLA_VAL_WIDTH), BF16),
        scratch_shapes=[pltpu.VMEM((s, 2 * GLA_DK), BF16),
                        pltpu.VMEM((s, GLA_DK), BF16),
                        pltpu.VMEM((s, GLA_DK), BF16),
                        pltpu.VMEM((s, GLA_DK), BF16),
                        pltpu.VMEM((n_blocks * n_cross, GLA_DK), BF16),
                        pltpu.VMEM((s, GLA_DK), BF16),
                        pltpu.VMEM((s, GLA_DK), BF16),
                        pltpu.VMEM((n_blocks, 8, GLA_DK), F32),
                        pltpu.VMEM((s, GLA_BLOCK_ROWS), BF16),
                        pltpu.VMEM((s, GLA_DV), F32),
                        pltpu.VMEM((n_blocks, GLA_DV, GLA_DK), F32),
                        pltpu.VMEM((n_blocks, GLA_DV, GLA_DK), BF16)],
        compiler_params=_params("parallel", "arbitrary"),
        name="gla",
    )(main, main, main, main, tail, w_a2, b_a.reshape(1, -1), norm_w.reshape(1, -1))


def _out_proj_kernel(*refs, n_parts, keep_residual):
    parts = refs[:n_parts]
    x_ref, w_ref, nw_ref = refs[n_parts:n_parts + 3]
    outs = refs[n_parts + 3:]
    for s0 in range(0, x_ref.shape[0], OUT_SUB_ROWS):
        rs = slice(s0, s0 + OUT_SUB_ROWS)
        y = x_ref[rs, :]
        r0 = 0
        for p in parts:
            width = p.shape[1]
            y = y + jnp.dot(p[rs, :], w_ref[r0:r0 + width, :], preferred_element_type=F32)
            r0 += width
        if keep_residual:
            y_ref, h_ref = outs
            y_ref[rs, :] = y
        else:
            (h_ref,) = outs
        h_ref[rs, :] = _rms(y, nw_ref[...]).astype(h_ref.dtype)


def _out_proj(parts, x, w, norm_w, *, keep_residual):
    t, d = x.shape
    rows = OUT_ROWS
    in_specs = [pl.BlockSpec((rows, p.shape[1]), lambda i: (i, 0)) for p in parts]
    in_specs += [pl.BlockSpec((rows, d), lambda i: (i, 0)),
                 pl.BlockSpec(w.shape, lambda i: (0, 0), pipeline_mode=pl.Buffered(1)),
                 pl.BlockSpec((1, d), lambda i: (0, 0))]
    args = list(parts) + [x, w, norm_w.reshape(1, d)]
    row_spec = pl.BlockSpec((rows, d), lambda i: (i, 0))
    if keep_residual:
        out_specs = [row_spec, row_spec]
        out_shape = [jax.ShapeDtypeStruct((t, d), F32), jax.ShapeDtypeStruct((t, d), BF16)]
    else:
        out_specs = row_spec
        out_shape = jax.ShapeDtypeStruct((t, d), F32)
    return pl.pallas_call(
        functools.partial(_out_proj_kernel, n_parts=len(parts), keep_residual=keep_residual),
        grid=(t // rows,),
        in_specs=in_specs,
        out_specs=out_specs,
        out_shape=out_shape,
        compiler_params=_params("parallel"),
        name="out_proj",
    )(*args)


def _memory_branch(main, mem2d, mem_norm_w, w_mem_kv, *, batch, q_col, g_col):
    n = w_mem_kv.shape[1]
    mem_kv = _norm_proj(mem2d, mem_norm_w, [w_mem_kv.astype(BF16)], jnp.ones((1, n), F32), None,
                        rows=2 * PROJ_ROWS)
    mem_kv = mem_kv.reshape(batch, -1, n)
    return _attention(main, mem_kv, mem_kv, main, None, heads=MEM_HEADS, dq=MEM_HEAD_DIM,
                      dv=MEM_HEAD_DIM, q_blk=q_col // MEM_HEAD_DIM, k_blk=0, v_blk=MEM_HEADS,
                      g_blk=g_col // MEM_HEAD_DIM, mask=None, q_rows=ATTN_Q_ROWS,
                      name="mem_attn")


def _col_scale(widths_and_scales):
    return jnp.concatenate([jnp.full((1, w), s, F32) for w, s in widths_and_scales], axis=1)


def _split_cols(w, sizes):
    out, acc = [], 0
    for s in sizes:
        out.append(w[:, acc:acc + s].astype(BF16))
        acc += s
    return out


def _even_layer(x2d, mem2d, pos_d, freq_row, batch, norm_w, w_in, fox_b_f,
                mla_q_norm_w, mla_w_uq, mla_kv_norm_w, mla_w_ukv, mem_norm_w, w_mem_kv, w_out,
                next_norm_w):
    t = x2d.shape[0]
    seq = t // batch
    (w_a, w_ff, w_b, w_kr, w_c) = _split_cols(
        w_in, (3 * FOX_WIDTH, FOX_HEADS, FOX_WIDTH + MLA_Q_RANK + MLA_KV_RANK,
               MLA_ROPE, MLA_WIDTH + 2 * MEM_WIDTH))
    fq_col, fk_col, fv_col = 0, FOX_WIDTH, 2 * FOX_WIDTH
    mg_col = 3 * FOX_WIDTH
    mq_col = mg_col + MLA_WIDTH
    memg_col = mq_col + MEM_WIDTH
    fg_col = memg_col + MEM_WIDTH
    cq_col = fg_col + FOX_WIDTH
    ckv_col = cq_col + MLA_Q_RANK
    n_main = ckv_col + MLA_KV_RANK
    ff_lane = MLA_ROPE
    w_tail = jnp.concatenate(
        [w_kr, w_ff, jnp.zeros((D_MODEL, LANES - MLA_ROPE - FOX_HEADS), BF16)], axis=1)
    col_scale = _col_scale([(FOX_WIDTH, FOX_HEAD_DIM ** -0.5 * LOG2E), (mq_col - FOX_WIDTH, 1.0),
                            (MEM_WIDTH, MEM_HEAD_DIM ** -0.5 * LOG2E),
                            (n_main - memg_col, 1.0)])
    main, tail = _norm_proj(x2d, norm_w, [w_a, w_c, w_b], col_scale, w_tail, rows=PROJ_ROWS)
    main3 = main.reshape(batch, seq, n_main)

    bias_row = jnp.zeros((1, LANES), F32).at[0, ff_lane:ff_lane + FOX_HEADS].set(fox_b_f)
    neg_c = _fox_gate(tail.reshape(batch, seq, LANES), bias_row, lane0=ff_lane, heads=FOX_HEADS)
    o_a = _attention(main3, main3, main3, main3, neg_c.reshape(batch * FOX_HEADS, 1, seq),
                     heads=FOX_HEADS, dq=FOX_HEAD_DIM, dv=FOX_HEAD_DIM,
                     q_blk=fq_col // FOX_HEAD_DIM, k_blk=fk_col // FOX_HEAD_DIM,
                     v_blk=fv_col // FOX_HEAD_DIM, g_blk=fg_col // FOX_HEAD_DIM,
                     mask="causal", q_rows=ATTN_Q_ROWS, name="fox_attn")

    pad = jnp.zeros((MLA_Q_RANK, MLA_HEADS, MLA_QK_PAD - MLA_NOPE - MLA_ROPE), F32)
    w_uq = jnp.concatenate([mla_w_uq.reshape(MLA_Q_RANK, MLA_HEADS, MLA_NOPE + MLA_ROPE), pad],
                           axis=2).reshape(MLA_Q_RANK, MLA_HEADS * MLA_QK_PAD).astype(BF16)
    w_ukv = mla_w_ukv.reshape(MLA_KV_RANK, MLA_HEADS, MLA_NOPE + MLA_V)
    w_ukv = jnp.concatenate([w_ukv[:, :, :MLA_NOPE].reshape(MLA_KV_RANK, -1),
                             w_ukv[:, :, MLA_NOPE:].reshape(MLA_KV_RANK, -1)], axis=1).astype(BF16)
    q_cat, k_cat, v_b = _mla_prep(main, tail, pos_d, freq_row, mla_q_norm_w,
                                  mla_kv_norm_w, w_uq, w_ukv, cq_col=cq_col, ckv_col=ckv_col)
    o_b = _attention(q_cat.reshape(batch, seq, -1), k_cat.reshape(batch, seq, -1),
                     v_b.reshape(batch, seq, -1), main3, None, heads=MLA_HEADS, dq=MLA_QK_PAD,
                     dv=MLA_V, q_blk=0, k_blk=0, v_blk=0, g_blk=mg_col // MLA_V,
                     mask="chunk", q_rows=ATTN_Q_ROWS, name="mla_attn")

    o_m = _memory_branch(main3, mem2d, mem_norm_w, w_mem_kv, batch=batch, q_col=mq_col,
                         g_col=memg_col)
    parts = [o_a.reshape(t, -1), o_b.reshape(t, -1), o_m.reshape(t, -1)]
    return _out_proj(parts, x2d, w_out.astype(BF16), next_norm_w, keep_residual=True)


def _odd_layer(x2d, h2d, mem2d, batch, w_in, gla_w_a2, gla_b_a, gla_norm_w, mem_norm_w,
               w_mem_kv, w_out, final_w):
    t = x2d.shape[0]
    seq = t // batch
    (w_a, w_ga, w_b) = _split_cols(
        w_in, (2 * GLA_KEY_WIDTH + GLA_VAL_WIDTH, GLA_GATE_RANK,
               GLA_VAL_WIDTH + 2 * MEM_WIDTH))
    mq_col = 2 * GLA_KEY_WIDTH + 2 * GLA_VAL_WIDTH
    memg_col = mq_col + MEM_WIDTH
    n_main = memg_col + MEM_WIDTH
    w_tail = jnp.concatenate([w_ga, jnp.zeros((D_MODEL, LANES - GLA_GATE_RANK), BF16)], axis=1)
    col_scale = _col_scale([(GLA_KEY_WIDTH, GLA_DK ** -0.5), (mq_col - GLA_KEY_WIDTH, 1.0),
                            (MEM_WIDTH, MEM_HEAD_DIM ** -0.5 * LOG2E), (MEM_WIDTH, 1.0)])
    main, tail = _norm_proj(h2d, None, [w_a, w_b], col_scale, w_tail, rows=2 * PROJ_ROWS)
    main3 = main.reshape(batch, seq, n_main)

    w_a2 = jnp.concatenate([gla_w_a2, jnp.zeros((LANES - GLA_GATE_RANK, GLA_KEY_WIDTH), F32)],
                           axis=0).astype(BF16)
    o_c = _gla(main3, tail.reshape(batch, seq, LANES), w_a2, gla_b_a, gla_norm_w)
    o_m = _memory_branch(main3, mem2d, mem_norm_w, w_mem_kv, batch=batch, q_col=mq_col,
                         g_col=memg_col)
    parts = [o_c.reshape(t, -1), o_m.reshape(t, -1)]
    return _out_proj(parts, x2d, w_out.astype(BF16), final_w, keep_residual=False)


def kernel(x, mem, positions, e_norm_w, e_w_in, e_fox_b_f, e_mla_q_norm_w, e_mla_w_uq, e_mla_kv_norm_w, e_mla_w_ukv, e_mem_norm_w, e_w_mem_kv, e_w_out, o_norm_w, o_w_in, o_gla_w_a2, o_gla_b_a, o_gla_norm_w, o_mem_norm_w, o_w_mem_kv, o_w_out, final_norm_w):
    batch, seq, d = x.shape
    t = batch * seq
    x2d = x.reshape(t, d)
    mem2d = mem.reshape(-1, d)
    inv_freq = 1.0 / (ROPE_THETA ** (jnp.arange(0, MLA_ROPE, 2, dtype=F32) / MLA_ROPE))
    half = MLA_ROPE // 2
    n_q = LANES // half
    freq_row = jnp.tile(inv_freq, n_q).reshape(1, LANES)
    pos_d = positions.astype(F32).reshape(t // PREP_ROWS, n_q, PREP_ROWS // n_q)
    pos_d = jnp.repeat(pos_d.transpose(0, 2, 1), half, axis=2).reshape(t // n_q, LANES)

    x2d, h2d = _even_layer(x2d, mem2d, pos_d, freq_row, batch, e_norm_w[0], e_w_in[0],
                           e_fox_b_f[0], e_mla_q_norm_w[0], e_mla_w_uq[0], e_mla_kv_norm_w[0],
                           e_mla_w_ukv[0], e_mem_norm_w[0], e_w_mem_kv[0], e_w_out[0],
                           o_norm_w[0])
    out = _odd_layer(x2d, h2d, mem2d, batch, o_w_in[0], o_gla_w_a2[0], o_gla_b_a[0],
                     o_gla_norm_w[0], o_mem_norm_w[0], o_w_mem_kv[0], o_w_out[0], final_norm_w)
    return out.reshape(batch, seq, d)
```
